```python
import math
import jax
import jax.numpy as jnp
from jax import lax
import numpy as np

D_MODEL = 1024
BATCH = 8
SEQ = 2048
DEPTH = 4
DEC_BATCH = 4
DEC_SEQ = 4096
PAST_LEN = 128

D_MIX = D_MODEL
HEAD_DIM = 64
RWKV_WIDTH = D_MIX // 2
RWKV_HEADS = RWKV_WIDTH // HEAD_DIM
DECAY_LORA = 64
AAA_LORA = 64
GATE_LORA = 128
SSM_WIDTH = D_MIX - RWKV_WIDTH
SSM_HEADS = SSM_WIDTH // HEAD_DIM
SSM_GROUPS = 2
D_STATE = 128
CONV_WIDTH = 5
CHUNK = 128
D_FF = 4 * D_MODEL
N_DIR = 2
N_MOD = 6
NORM_EPS = 1e-6
GN_EPS = 64e-5

RWKV_COLS = 3 * RWKV_WIDTH + N_DIR * DECAY_LORA + N_DIR * AAA_LORA + GATE_LORA
CONV_COLS = SSM_WIDTH + 2 * SSM_GROUPS * D_STATE
SSM_COLS = SSM_WIDTH + CONV_COLS + N_DIR * SSM_HEADS
D_IN_PROJ = RWKV_COLS + SSM_COLS

kernel_name = "hymba_rwkv7_mamba2_biencoder"


def rmsnorm(x, gain):
    xf = x.astype(jnp.float32)
    y = xf * lax.rsqrt(jnp.mean(xf * xf, axis=-1, keepdims=True) + NORM_EPS)
    return (y * gain.astype(jnp.float32)).astype(x.dtype)


def split_heads(z):
    return z.reshape(z.shape[:-1] + (-1, HEAD_DIM))


def centred_token_shift(u, mu_prev, mu_next):
    prev = jnp.pad(u[:, :-1], ((0, 0), (1, 0), (0, 0)))
    nxt = jnp.pad(u[:, 1:], ((0, 0), (0, 1), (0, 0)))
    return u + mu_prev * (prev - u) + mu_next * (nxt - u)


def centred_depthwise_conv(u, w, b):
    half = CONV_WIDTH // 2
    out = lax.conv_general_dilated(
        u, w[:, None, :].astype(u.dtype), window_strides=(1,), padding=[(half, half)],
        dimension_numbers=("NWC", "WIO", "NWC"), feature_group_count=u.shape[-1])
    return out + b


def rwkv7_scan(r, w, k, v, kk, a, reverse):
    b, t, h, n = r.shape
    xs = tuple(jnp.moveaxis(z, 1, 0) for z in (r, w, k, v, kk, a))

    def step(S, inp):
        r_t, w_t, k_t, v_t, kk_t, a_t = inp
        sa = jnp.einsum("bhvk,bhk->bhv", S, kk_t)
        S = (S * w_t[:, :, None, :]
             - sa[..., None] * (kk_t * a_t)[:, :, None, :]
             + v_t[..., None] * k_t[:, :, None, :])
        return S, jnp.einsum("bhvk,bhk->bhv", S, r_t)

    S0 = jnp.zeros((b, h, n, n), jnp.float32)
    _, y = lax.scan(step, S0, xs, reverse=reverse)
    return jnp.moveaxis(y, 0, 1)


def rwkv7_mixer(u, mu, w0, w_up, a0, a_up, g_up, k_k, k_a, r_k, gn_w, gn_b):
    b, t, _ = u.shape
    u = centred_token_shift(u, mu[0], mu[1]).astype(jnp.float32)
    offs = [RWKV_WIDTH, 2 * RWKV_WIDTH, 3 * RWKV_WIDTH,
            3 * RWKV_WIDTH + N_DIR * DECAY_LORA,
            3 * RWKV_WIDTH + N_DIR * DECAY_LORA + N_DIR * AAA_LORA]
    r, k, v, w_lo, a_lo, g_lo = jnp.split(u, offs, axis=-1)
    w_lo = w_lo.reshape(b, t, N_DIR, DECAY_LORA)
    a_lo = a_lo.reshape(b, t, N_DIR, AAA_LORA)
    w_raw = w0 + jnp.einsum("btdr,drc->btdc", jnp.tanh(w_lo), w_up)
    decay = jnp.exp(-jnp.exp(-jax.nn.softplus(-w_raw) - 0.5))
    a = jax.nn.sigmoid(a0 + jnp.einsum("btdr,drc->btdc", a_lo, a_up))
    g = jax.nn.sigmoid(g_lo) @ g_up
    rh, vh = split_heads(r), split_heads(v)
    kk = split_heads(k * k_k)
    kk = kk / jnp.maximum(jnp.sqrt(jnp.sum(kk * kk, axis=-1, keepdims=True)), 1e-12)
    y = jnp.zeros_like(rh)
    for d in range(N_DIR):
        k_d = k * (1.0 + (a[:, :, d] - 1.0) * k_a)
        y = y + rwkv7_scan(rh, split_heads(decay[:, :, d]), split_heads(k_d), vh, kk,
                           split_heads(a[:, :, d]), reverse=(d == 1))
    mean = jnp.mean(y, axis=-1, keepdims=True)
    var = jnp.mean(jnp.square(y - mean), axis=-1, keepdims=True)
    y = ((y - mean) * lax.rsqrt(var + GN_EPS)).reshape(b, t, RWKV_WIDTH) * gn_w + gn_b
    bonus = jnp.sum(rh * split_heads(k) * r_k, axis=-1, keepdims=True) * vh
    y = y + bonus.reshape(b, t, RWKV_WIDTH)
    return y * g


def ssd_chunked(x, dt, A, B, C):
    x, dt, B, C = (z.astype(jnp.float32) for z in (x, dt, B, C))
    A = A.astype(jnp.float32)
    b, t, h, p = x.shape
    g, n = B.shape[-2:]
    e = h // g
    nc = t // CHUNK
    xc = (x * dt[..., None]).reshape(b, nc, CHUNK, g, e, p)
    Ac = jnp.transpose((dt * A).reshape(b, nc, CHUNK, g, e), (0, 3, 4, 1, 2))
    Bc = B.reshape(b, nc, CHUNK, g, n)
    Cc = C.reshape(b, nc, CHUNK, g, n)
    A_cum = jnp.cumsum(Ac, axis=-1)
    lower = jnp.tril(jnp.ones((CHUNK, CHUNK), dtype=bool))
    seg = A_cum[..., :, None] - A_cum[..., None, :]
    Lmat = jnp.exp(jnp.where(lower, seg, -jnp.inf))
    CB = jnp.einsum("bclgn,bcsgn->bgcls", Cc, Bc)
    y_diag = jnp.einsum("bgcls,bgecls,bcsgep->bclgep", CB, Lmat, xc)
    decay_states = jnp.exp(A_cum[..., -1:] - A_cum)
    states = jnp.einsum("bclgn,bgecl,bclgep->bcgepn", Bc, decay_states, xc)
    chunk_decay = jnp.exp(A_cum[..., -1])

    def step(S, inp):
        st, dec = inp
        return S * dec[..., None, None] + st, S

    S0 = jnp.zeros((b, g, e, p, n), jnp.float32)
    _, prev = lax.scan(step, S0, (jnp.moveaxis(states, 1, 0), jnp.moveaxis(chunk_decay, 3, 0)))
    prev = jnp.moveaxis(prev, 0, 1)
    y_off = jnp.einsum("bclgn,bcgepn,bgecl->bclgep", Cc, prev, jnp.exp(A_cum))
    return (y_diag + y_off).reshape(b, t, h, p)


def mamba2_mixer(u, conv_w, conv_b, dt_bias, A_log, d_skip, norm_w):
    b, t, _ = u.shape
    z, xBC, dt_raw = jnp.split(u, [SSM_WIDTH, SSM_WIDTH + CONV_COLS], axis=-1)
    xBC = jax.nn.silu(centred_depthwise_conv(xBC, conv_w, conv_b))
    xs, Bm, Cm = jnp.split(xBC, [SSM_WIDTH, SSM_WIDTH + SSM_GROUPS * D_STATE], axis=-1)
    xs = xs.reshape(b, t, SSM_HEADS, HEAD_DIM)
    Bm = Bm.reshape(b, t, SSM_GROUPS, D_STATE)
    Cm = Cm.reshape(b, t, SSM_GROUPS, D_STATE)
    dt_raw = dt_raw.astype(jnp.float32).reshape(b, t, N_DIR, SSM_HEADS)
    A = -jnp.exp(A_log.astype(jnp.float32))
    flip = lambda q: jnp.flip(q, axis=1)
    dt_f = jax.nn.softplus(dt_raw[:, :, 0] + dt_bias[0])
    dt_b = jax.nn.softplus(dt_raw[:, :, 1] + dt_bias[1])
    y = ssd_chunked(xs, dt_f, A[0], Bm, Cm)
    y = y + flip(ssd_chunked(flip(xs), flip(dt_b), A[1], flip(Bm), flip(Cm)))
    y = y + d_skip[:, None] * xs
    y = y.reshape(b, t, SSM_WIDTH)
    return rmsnorm(y * jax.nn.silu(z.astype(jnp.float32)), norm_w)


def encoder_trunk(x, c, weights):
    (w_mod, b_mod, norm_g, w_in, shift_mu, w0, w_up, a0, a_up, g_up, k_k, k_a, r_k,
     gn_w, gn_b, conv_w, conv_b, dt_bias, A_log, d_skip, ssm_norm_w, w_out, w_ff1, w_ff2) = weights
    nb = c.shape[0]
    for l in range(DEPTH):
        mod = (jax.nn.silu(c) @ w_mod[l] + b_mod[l]).reshape(nb, N_MOD, D_MODEL)[:, :, None, :]
        sh_m, sc_m, gt_m, sh_f, sc_f, gt_f = (mod[:, i] for i in range(N_MOD))
        h = rmsnorm(x, norm_g[l, 0]) * (1.0 + sc_m) + sh_m
        u = h @ w_in[l]
        y_r = rwkv7_mixer(u[..., :RWKV_COLS], shift_mu[l], w0[l], w_up[l], a0[l], a_up[l],
                          g_up[l], k_k[l], k_a[l], r_k[l], gn_w[l], gn_b[l])
        y_s = mamba2_mixer(u[..., RWKV_COLS:], conv_w[l], conv_b[l], dt_bias[l], A_log[l],
                           d_skip[l], ssm_norm_w[l])
        o = jnp.concatenate([y_r, y_s], axis=-1).astype(h.dtype) @ w_out[l]
        x = x + gt_m * rmsnorm(o, norm_g[l, 1])
        h = rmsnorm(x, norm_g[l, 2]) * (1.0 + sc_f) + sh_f
        f = jnp.square(jax.nn.relu(h @ w_ff1[l])) @ w_ff2[l]
        x = x + gt_f * rmsnorm(f, norm_g[l, 3])
    return x


def setup_inputs(seed: int = 0) -> dict:
    key = jax.random.key(seed)
    ks = jax.random.split(key, 28)
    nrm = lambda k, shape, s: s * jax.random.normal(k, shape, jnp.float32)
    L = DEPTH
    x_prompt = nrm(ks[0], (BATCH, SEQ, D_MODEL), 1.0)
    x_sample = nrm(ks[1], (DEC_BATCH, DEC_SEQ, D_MODEL), 1.0)
    c_prompt = nrm(ks[2], (BATCH, D_MODEL), 1.0)
    c_sample = nrm(ks[3], (DEC_BATCH, D_MODEL), 1.0)
    w_mod = nrm(ks[4], (L, D_MODEL, N_MOD * D_MODEL), 0.5 * D_MODEL ** -0.5)
    b_mod = nrm(ks[5], (L, N_MOD * D_MODEL), 0.02)
    norm_g = 1.0 + nrm(ks[6], (L, 4, D_MODEL), 0.05)
    w_in = nrm(ks[7], (L, D_MODEL, D_IN_PROJ), D_MODEL ** -0.5)
    shift_mu = jax.random.uniform(ks[8], (L, 2, RWKV_COLS), jnp.float32, 0.05, 0.45)
    decay_base = jnp.linspace(-6.0, -1.0, RWKV_WIDTH, dtype=jnp.float32)
    w0 = decay_base + nrm(ks[9], (L, N_DIR, RWKV_WIDTH), 0.1)
    w_up = nrm(ks[10], (L, N_DIR, DECAY_LORA, RWKV_WIDTH), 0.5 * DECAY_LORA ** -0.5)
    a0 = nrm(ks[11], (L, N_DIR, RWKV_WIDTH), 0.1)
    a_up = nrm(ks[12], (L, N_DIR, AAA_LORA, RWKV_WIDTH), 0.5 * AAA_LORA ** -0.5)
    g_up = nrm(ks[13], (L, GATE_LORA, RWKV_WIDTH), GATE_LORA ** -0.5)
    k_k = 0.85 + nrm(ks[14], (L, RWKV_WIDTH), 0.02)
    k_a = 1.0 + nrm(ks[15], (L, RWKV_WIDTH), 0.02)
    r_k = nrm(ks[16], (L, RWKV_HEADS, HEAD_DIM), 0.1)
    gn_w = 1.0 + nrm(ks[17], (L, RWKV_WIDTH), 0.05)
    gn_b = nrm(ks[18], (L, RWKV_WIDTH), 0.02)
    conv_w = nrm(ks[19], (L, CONV_WIDTH, CONV_COLS), CONV_WIDTH ** -0.5)
    conv_b = nrm(ks[20], (L, CONV_COLS), 0.02)
    dt0 = jnp.exp(jax.random.uniform(ks[21], (L, N_DIR, SSM_HEADS), jnp.float32,
                                     math.log(1e-3), math.log(1e-1)))
    dt_bias = dt0 + jnp.log(-jnp.expm1(-dt0))
    A_log = jnp.log(jax.random.uniform(ks[22], (L, N_DIR, SSM_HEADS), jnp.float32, 1.0, 16.0))
    d_skip = 1.0 + nrm(ks[23], (L, SSM_HEADS), 0.1)
    ssm_norm_w = 1.0 + nrm(ks[24], (L, SSM_WIDTH), 0.05)
    w_out = nrm(ks[25], (L, D_MIX, D_MODEL), D_MIX ** -0.5)
    w_ff1 = nrm(ks[26], (L, D_MODEL, D_FF), D_MODEL ** -0.5)
    w_ff2 = nrm(ks[27], (L, D_FF, D_MODEL), D_FF ** -0.5)
    return {"x_prompt": x_prompt, "x_sample": x_sample, "c_prompt": c_prompt, "c_sample": c_sample,
            "w_mod": w_mod, "b_mod": b_mod, "norm_g": norm_g, "w_in": w_in, "shift_mu": shift_mu,
            "w0": w0, "w_up": w_up, "a0": a0, "a_up": a_up, "g_up": g_up, "k_k": k_k, "k_a": k_a,
            "r_k": r_k, "gn_w": gn_w, "gn_b": gn_b, "conv_w": conv_w, "conv_b": conv_b,
            "dt_bias": dt_bias, "A_log": A_log, "d_skip": d_skip, "ssm_norm_w": ssm_norm_w,
            "w_out": w_out, "w_ff1": w_ff1, "w_ff2": w_ff2}


def reference(x_prompt, x_sample, c_prompt, c_sample, w_mod, b_mod, norm_g, w_in, shift_mu,
              w0, w_up, a0, a_up, g_up, k_k, k_a, r_k, gn_w, gn_b, conv_w, conv_b,
              dt_bias, A_log, d_skip, ssm_norm_w, w_out, w_ff1, w_ff2):
    weights = (w_mod, b_mod, norm_g, w_in, shift_mu, w0, w_up, a0, a_up, g_up, k_k, k_a, r_k,
               gn_w, gn_b, conv_w, conv_b, dt_bias, A_log, d_skip, ssm_norm_w, w_out, w_ff1, w_ff2)
    y_prompt = encoder_trunk(x_prompt, c_prompt, weights)
    y_sample = encoder_trunk(x_sample, c_sample, weights)
    return (y_prompt, y_sample)
```

```python
import functools
import math

import jax
import jax.numpy as jnp
from jax import lax
from jax.experimental import pallas as pl
from jax.experimental.pallas import tpu as pltpu

F32 = jnp.float32
BF16 = jnp.bfloat16
HIGHEST = lax.Precision.HIGHEST

D_MODEL = 1024
HEAD_DIM = 64
RW = 512
N_HEADS = RW // HEAD_DIM
LORA = 64
GATE_LORA = 128
SW = 512
SSM_GROUPS = 2
D_STATE = 128
CONV_WIDTH = 5
SSD_CHUNK = 128
RWKV_CHUNK = 64
D_FF = 4 * D_MODEL
N_MOD = 6
NORM_EPS = 1e-6
GN_EPS = 64e-5
RWKV_COLS = 3 * RW + 2 * LORA + 2 * LORA + GATE_LORA
XBC_COLS = SW + 2 * SSM_GROUPS * D_STATE
HALO = 8
VMEM_LIMIT = 56 * 1024 * 1024
DECAY_SCALE = math.exp(-0.5)


def _bdot(a, b):
    return jnp.dot(a.astype(BF16), b.astype(BF16), preferred_element_type=F32)


def _bdot_nt(a, b):
    return lax.dot_general(a.astype(BF16), b.astype(BF16), (((1,), (1,)), ((), ())),
                           preferred_element_type=F32)


def _bdot_tn(a, b):
    return lax.dot_general(a.astype(BF16), b.astype(BF16), (((0,), (0,)), ((), ())),
                           preferred_element_type=F32)


def _fdot(a, b):
    return jnp.dot(a, b, precision=HIGHEST, preferred_element_type=F32)


def _fdot_nt(a, b):
    return lax.dot_general(a, b, (((1,), (1,)), ((), ())), precision=HIGHEST,
                           preferred_element_type=F32)


def _sigmoid(x):
    return 1.0 / (1.0 + jnp.exp(-x))


def _softplus(x):
    return jnp.maximum(x, 0.0) + jnp.log1p(jnp.exp(-jnp.abs(x)))


def _rms(x, gain):
    return x * lax.rsqrt(jnp.mean(x * x, axis=-1, keepdims=True) + NORM_EPS) * gain


def _params(*sem):
    return pltpu.CompilerParams(dimension_semantics=sem, vmem_limit_bytes=VMEM_LIMIT)


def _full(shape):
    return pl.BlockSpec(shape, lambda *_: (0,) * len(shape))


def _tok(tm, width, col=0):
    return pl.BlockSpec((1, tm, width), lambda b, i: (b, i, col))


def _halo_prev(tm, width):
    return pl.BlockSpec((1, HALO, width), lambda b, i: (b, jnp.maximum(i * (tm // HALO) - 1, 0), 0))


def _halo_next(tm, width, t):
    return pl.BlockSpec((1, HALO, width),
                        lambda b, i: (b, jnp.minimum((i + 1) * (tm // HALO), t // HALO - 1), 0))


def _mod_kernel(c_ref, w_ref, b_ref, o_ref):
    c = c_ref[...]
    o_ref[0] = _fdot(c * _sigmoid(c), w_ref[0]) + b_ref[0]


def _modulation(c_all, w_mod, b_mod):
    depth = w_mod.shape[0]
    rows = c_all.shape[0]
    nblk = N_MOD
    return pl.pallas_call(
        _mod_kernel,
        grid=(depth, nblk),
        in_specs=[pl.BlockSpec((rows, D_MODEL), lambda l, j: (0, 0)),
                  pl.BlockSpec((1, D_MODEL, D_MODEL), lambda l, j: (l, 0, j)),
                  pl.BlockSpec((1, 1, D_MODEL), lambda l, j: (l, 0, j))],
        out_specs=pl.BlockSpec((1, rows, D_MODEL), lambda l, j: (l, 0, j)),
        out_shape=jax.ShapeDtypeStruct((depth, rows, N_MOD * D_MODEL), F32),
        compiler_params=_params("arbitrary", "arbitrary"),
        name="modulation",
    )(c_all, w_mod, b_mod.reshape(depth, 1, N_MOD * D_MODEL))


def _inproj_kernel(x_ref, mod_ref, g_ref, wr_ref, wz_ref, wx_ref, wdt_ref, wdtt_ref, db_ref, dbt_ref,
                   ur_ref, z_ref, xbc_ref, dt_ref, dtt_ref):
    m = mod_ref[0]
    h = _rms(x_ref[0], g_ref[...]) * (1.0 + m[1:2, :]) + m[0:1, :]
    hb = h.astype(BF16)
    ur_ref[0] = jnp.dot(hb, wr_ref[...], preferred_element_type=F32)
    z_ref[0] = jnp.dot(hb, wz_ref[...], preferred_element_type=F32)
    xbc_ref[0] = jnp.dot(hb, wx_ref[...], preferred_element_type=F32)
    dt_ref[0] = _softplus(jnp.dot(hb, wdt_ref[...], preferred_element_type=F32) + db_ref[...])
    dtt_ref[0] = _softplus(_bdot_nt(wdtt_ref[...], hb) + dbt_ref[...])


def _inproj(x, mod, gain, wr, wz, wx, wdt, wdtt, db, dbt, tm):
    nb, t, _ = x.shape
    nd = wdt.shape[1]
    return pl.pallas_call(
        _inproj_kernel,
        grid=(nb, t // tm),
        in_specs=[_tok(tm, D_MODEL),
                  pl.BlockSpec((1, N_MOD, D_MODEL), lambda b, i: (b, 0, 0)),
                  _full((1, D_MODEL)), _full(wr.shape), _full(wz.shape), _full(wx.shape),
                  _full(wdt.shape), _full(wdtt.shape), _full(db.shape), _full(dbt.shape)],
        out_specs=[_tok(tm, RWKV_COLS), _tok(tm, SW), _tok(tm, XBC_COLS), _tok(tm, nd),
                   pl.BlockSpec((1, nd, tm), lambda b, i: (b, 0, i))],
        out_shape=[jax.ShapeDtypeStruct((nb, t, RWKV_COLS), F32),
                   jax.ShapeDtypeStruct((nb, t, SW), F32),
                   jax.ShapeDtypeStruct((nb, t, XBC_COLS), F32),
                   jax.ShapeDtypeStruct((nb, t, nd), F32),
                   jax.ShapeDtypeStruct((nb, nd, t), F32)],
        compiler_params=_params("parallel", "parallel"),
        name="inproj",
    )(x, mod, gain, wr, wz, wx, wdt, wdtt, db, dbt)


def _fill_halo(ext_ref, main_ref, prev_ref, next_ref, tm):
    i = pl.program_id(1)
    ext_ref[pl.ds(HALO, tm), :] = main_ref[0]
    ext_ref[pl.ds(0, HALO), :] = jnp.where(i > 0, prev_ref[0], 0.0)
    ext_ref[pl.ds(HALO + tm, HALO), :] = jnp.where(i < pl.num_programs(1) - 1, next_ref[0], 0.0)


def _rwkv_prep_kernel(u_ref, up_ref, un_ref, mu_ref, w0_ref, wup_ref, a0_ref, aup_ref, gup_ref,
                      kk_ref, rk_ref, bd_ref,
                      r_out, k_out, v_out, kkn_out, g_out, bonus_out, lw_out, a_out, ext_ref):
    tm = u_ref.shape[1]
    _fill_halo(ext_ref, u_ref, up_ref, un_ref, tm)
    u = ext_ref[pl.ds(HALO, tm), :]
    prev = ext_ref[pl.ds(HALO - 1, tm), :]
    nxt = ext_ref[pl.ds(HALO + 1, tm), :]
    mu = mu_ref[...]
    us = u + mu[0:1, :] * (prev - u) + mu[1:2, :] * (nxt - u)
    r = us[:, 0:RW]
    k = us[:, RW:2 * RW]
    v = us[:, 2 * RW:3 * RW]
    o = 3 * RW
    w_lo = us[:, o:o + 2 * LORA]
    a_lo = us[:, o + 2 * LORA:o + 4 * LORA]
    g_lo = us[:, o + 4 * LORA:o + 4 * LORA + GATE_LORA]
    w_raw = w0_ref[...] + _bdot(jnp.tanh(w_lo), wup_ref[...])
    lw_out[0] = -DECAY_SCALE * _sigmoid(w_raw)
    a_out[0] = _sigmoid(a0_ref[...] + _bdot(a_lo, aup_ref[...]))
    g_out[0] = _bdot(_sigmoid(g_lo), gup_ref[...])
    bd = bd_ref[...]
    kk = k * kk_ref[...]
    kk_norm = jnp.sqrt(_fdot(kk * kk, bd))
    kkn_out[0] = kk / jnp.maximum(kk_norm, 1e-12)
    bonus_out[0] = _fdot(r * k * rk_ref[...], bd) * v
    r_out[0] = r
    k_out[0] = k
    v_out[0] = v


def _rwkv_prep(u_r, mu, w0, wup, a0, aup, gup, k_k, r_k, bd, tm):
    nb, t, _ = u_r.shape
    wide = jax.ShapeDtypeStruct((nb, t, 2 * RW), F32)
    one = jax.ShapeDtypeStruct((nb, t, RW), F32)
    return pl.pallas_call(
        _rwkv_prep_kernel,
        grid=(nb, t // tm),
        in_specs=[_tok(tm, RWKV_COLS), _halo_prev(tm, RWKV_COLS), _halo_next(tm, RWKV_COLS, t),
                  _full(mu.shape), _full(w0.shape), _full(wup.shape), _full(a0.shape),
                  _full(aup.shape), _full(gup.shape), _full(k_k.shape), _full(r_k.shape),
                  _full(bd.shape)],
        out_specs=[_tok(tm, RW)] * 6 + [_tok(tm, 2 * RW)] * 2,
        out_shape=[one] * 6 + [wide] * 2,
        scratch_shapes=[pltpu.VMEM((tm + 2 * HALO, RWKV_COLS), F32)],
        compiler_params=_params("parallel", "parallel"),
        name="rwkv_prep",
    )(u_r, u_r, u_r, mu, w0, wup, a0, aup, gup, k_k, r_k, bd)


def _rwkv_scan_kernel(rf, kf, vf, kkf, lwf, af, rb, kb, vb, kkb, lwb, ab, ka_ref,
                      yf_ref, yb_ref, s_ref):
    @pl.when(pl.program_id(1) == 0)
    def _():
        s_ref[...] = jnp.zeros_like(s_ref)

    c = rf.shape[1]
    row = lax.broadcasted_iota(jnp.int32, (c, c), 0)
    col = lax.broadcasted_iota(jnp.int32, (c, c), 1)
    eye = (row == col).astype(F32)
    ka = ka_ref[...]
    dirs = ((rf, kf, vf, kkf, lwf, af, yf_ref), (rb, kb, vb, kkb, lwb, ab, yb_ref))
    for d, (r_ref, k_ref, v_ref, kk_ref, lw_ref, a_ref, y_ref) in enumerate(dirs):
        incl = (col <= row) if d == 0 else (col >= row)
        strict = (col < row) if d == 0 else (col > row)
        last = c - 1 if d == 0 else 0
        r, k, v, kk, lw, a = r_ref[0], k_ref[0], v_ref[0], kk_ref[0], lw_ref[0], a_ref[0]
        cum = _fdot(incl.astype(F32), lw)
        cum_end = cum[last:last + 1, :]
        e_neg = jnp.exp(-cum)
        e_end = jnp.exp(cum_end - cum)
        kd = k * (1.0 + (a - 1.0) * ka)
        bb = kk * a
        at = -kk * jnp.exp(cum - lw)
        rt = r * jnp.exp(cum)
        bt = bb * e_neg
        kt = kd * e_neg
        bh = bb * e_end
        kh = kd * e_end
        p_end = jnp.exp(cum_end)
        for h in range(N_HEADS):
            sl = slice(h * HEAD_DIM, (h + 1) * HEAD_DIM)
            lhs = jnp.concatenate([at[:, sl], rt[:, sl]], axis=0)
            ab_ = _bdot_nt(lhs, bt[:, sl])
            ak_ = _bdot_nt(lhs, kt[:, sl])
            n_ab = jnp.where(strict, ab_[:c], 0.0)
            a_ak = jnp.where(strict, ak_[:c], 0.0)
            a_rb = jnp.where(incl, ab_[c:], 0.0)
            a_rk = jnp.where(incl, ak_[c:], 0.0)
            inv = eye + n_ab
            pw = n_ab
            for _ in range(int(math.log2(c)) - 1):
                pw = _fdot(pw, pw)
                inv = inv + _fdot(inv, pw)
            s0 = s_ref[d, h]
            sh = _bdot_nt(lhs, s0)
            vh = v[:, sl]
            u = _fdot(inv, sh[:c] + _bdot(a_ak, vh))
            y_ref[0, :, sl] = sh[c:] + _bdot(a_rb, u) + _bdot(a_rk, vh)
            s_ref[d, h] = s0 * p_end[:, sl] + _bdot_tn(u, bh[:, sl]) + _bdot_tn(vh, kh[:, sl])


def _rwkv_scan(r, k, v, kk, lw, a, k_a):
    nb, t, _ = r.shape
    c = RWKV_CHUNK
    nc = t // c
    fwd = pl.BlockSpec((1, c, RW), lambda b, i: (b, i, 0))
    bwd = pl.BlockSpec((1, c, RW), lambda b, i: (b, nc - 1 - i, 0))
    bwd1 = pl.BlockSpec((1, c, RW), lambda b, i: (b, nc - 1 - i, 1))
    out = jax.ShapeDtypeStruct((nb, t, RW), F32)
    return pl.pallas_call(
        _rwkv_scan_kernel,
        grid=(nb, nc),
        in_specs=[fwd] * 6 + [bwd] * 4 + [bwd1] * 2 + [_full(k_a.shape)],
        out_specs=[fwd, bwd],
        out_shape=[out, out],
        scratch_shapes=[pltpu.VMEM((2, N_HEADS, HEAD_DIM, HEAD_DIM), F32)],
        compiler_params=_params("parallel", "arbitrary"),
        name="rwkv_scan",
    )(r, k, v, kk, lw, a, r, k, v, kk, lw, a, k_a)


def _ssd_prep_kernel(u_ref, up_ref, un_ref, w_ref, b_ref, o_ref, ext_ref):
    tm = u_ref.shape[1]
    _fill_halo(ext_ref, u_ref, up_ref, un_ref, tm)
    w = w_ref[...]
    half = CONV_WIDTH // 2
    acc = b_ref[...] + w[0:1, :] * ext_ref[pl.ds(HALO - half, tm), :]
    for j in range(1, CONV_WIDTH):
        acc = acc + w[j:j + 1, :] * ext_ref[pl.ds(HALO - half + j, tm), :]
    o_ref[0] = acc * _sigmoid(acc)


def _ssd_prep(xbc, conv_w, conv_b, tm):
    nb, t, _ = xbc.shape
    return pl.pallas_call(
        _ssd_prep_kernel,
        grid=(nb, t // tm),
        in_specs=[_tok(tm, XBC_COLS), _halo_prev(tm, XBC_COLS), _halo_next(tm, XBC_COLS, t),
                  _full(conv_w.shape), _full(conv_b.shape)],
        out_specs=_tok(tm, XBC_COLS),
        out_shape=jax.ShapeDtypeStruct((nb, t, XBC_COLS), F32),
        scratch_shapes=[pltpu.VMEM((tm + 2 * HALO, XBC_COLS), F32)],
        compiler_params=_params("parallel", "parallel"),
        name="ssd_prep",
    )(xbc, xbc, xbc, conv_w, conv_b)


def _ssd_scan_kernel(xf, dtf, dttf, xb, dtb, dttb, alog_ref, alogt_ref, yf_ref, yb_ref, s_ref):
    @pl.when(pl.program_id(1) == 0)
    def _():
        s_ref[...] = jnp.zeros_like(s_ref)

    n = xf.shape[1]
    row = lax.broadcasted_iota(jnp.int32, (n, n), 0)
    col = lax.broadcasted_iota(jnp.int32, (n, n), 1)
    expand = (lax.broadcasted_iota(jnp.int32, (N_HEADS, SW), 1) // HEAD_DIM
              == lax.broadcasted_iota(jnp.int32, (N_HEADS, SW), 0)).astype(F32)
    gw = SW // SSM_GROUPS
    hpg = N_HEADS // SSM_GROUPS
    dirs = ((xf, dtf, dttf, yf_ref), (xb, dtb, dttb, yb_ref))
    for d, (x_ref, dt_ref, dtt_ref, y_ref) in enumerate(dirs):
        incl = (col <= row) if d == 0 else (col >= row)
        last = n - 1 if d == 0 else 0
        tri = incl.astype(F32)
        xbc = x_ref[0]
        xs = xbc[:, 0:SW]
        dt = dt_ref[0][:, d * N_HEADS:(d + 1) * N_HEADS]
        dtt = dtt_ref[0][d * N_HEADS:(d + 1) * N_HEADS, :]
        a_row = -jnp.exp(alog_ref[d:d + 1, :])
        a_col = -jnp.exp(alogt_ref[d * N_HEADS:(d + 1) * N_HEADS, :])
        cum = _fdot(tri, dt * a_row)
        cum_t = _fdot_nt(dtt * a_col, tri)
        cum_end = cum[last:last + 1, :]
        dtx = _fdot(dt, expand)
        decx = _fdot(jnp.exp(cum_end - cum), expand)
        ecumx = _fdot(jnp.exp(cum), expand)
        xdt = xs * dtx
        xdec = xdt * decx
        for g in range(SSM_GROUPS):
            bg = xbc[:, SW + g * D_STATE:SW + (g + 1) * D_STATE]
            cg = xbc[:, SW + SSM_GROUPS * D_STATE + g * D_STATE:SW + SSM_GROUPS * D_STATE + (g + 1) * D_STATE]
            gl = slice(g * gw, (g + 1) * gw)
            cb = _bdot_nt(cg, bg)
            s0 = s_ref[d, g]
            y_off = _bdot(cg, s0) * ecumx[:, gl]
            for e in range(hpg):
                h = g * hpg + e
                sl = slice(h * HEAD_DIM, (h + 1) * HEAD_DIM)
                seg = cum[:, h:h + 1] - cum_t[h:h + 1, :]
                lmat = jnp.where(incl, jnp.exp(seg), 0.0)
                y_ref[0, :, sl] = _bdot(cb * lmat, xdt[:, sl]) + y_off[:, e * HEAD_DIM:(e + 1) * HEAD_DIM]
            s_ref[d, g] = s0 * ecumx[last:last + 1, gl] + _bdot_tn(bg, xdec[:, gl])


def _ssd_scan(xbc, dt, dtt, a_log, a_log_t):
    nb, t, _ = xbc.shape
    n = SSD_CHUNK
    nc = t // n
    nd = dt.shape[2]
    specs = []
    for idx in (lambda i: i, lambda i: nc - 1 - i):
        specs += [pl.BlockSpec((1, n, XBC_COLS), lambda b, i, f=idx: (b, f(i), 0)),
                  pl.BlockSpec((1, n, nd), lambda b, i, f=idx: (b, f(i), 0)),
                  pl.BlockSpec((1, nd, n), lambda b, i, f=idx: (b, 0, f(i)))]
    out = jax.ShapeDtypeStruct((nb, t, SW), F32)
    return pl.pallas_call(
        _ssd_scan_kernel,
        grid=(nb, nc),
        in_specs=specs + [_full(a_log.shape), _full(a_log_t.shape)],
        out_specs=[pl.BlockSpec((1, n, SW), lambda b, i: (b, i, 0)),
                   pl.BlockSpec((1, n, SW), lambda b, i: (b, nc - 1 - i, 0))],
        out_shape=[out, out],
        scratch_shapes=[pltpu.VMEM((2, SSM_GROUPS, D_STATE, SW // SSM_GROUPS), F32)],
        compiler_params=_params("parallel", "arbitrary"),
        name="ssd_scan",
    )(xbc, dt, dtt, xbc, dt, dtt, a_log, a_log_t)


def _outproj_kernel(x_ref, mod_ref, yrf, yrb, bonus, gate_r, ysf, ysb, xs, z, bd_ref, gnw, gnb,
                    dsk, snw, wout_ref, ng_ref, o_ref):
    bd = bd_ref[...]
    inv_n = 1.0 / HEAD_DIM
    y = yrf[0] + yrb[0]
    delta = y - _fdot(y, bd) * inv_n
    var = _fdot(delta * delta, bd) * inv_n
    y_r = (delta * lax.rsqrt(var + GN_EPS) * gnw[...] + gnb[...] + bonus[0]) * gate_r[0]
    zz = z[0]
    y_s = (ysf[0] + ysb[0] + dsk[...] * xs[0]) * (zz * _sigmoid(zz))
    y_s = _rms(y_s, snw[...])
    o = _bdot(y_r, wout_ref[0:RW, :]) + _bdot(y_s, wout_ref[RW:RW + SW, :])
    o_ref[0] = x_ref[0] + mod_ref[0][2:3, :] * _rms(o, ng_ref[...])


def _outproj(x, mod, yrf, yrb, bonus, gate_r, ysf, ysb, xbc_act, z, bd, gn_w, gn_b, dsk, snw,
             wout, gain, tm):
    nb, t, _ = x.shape
    return pl.pallas_call(
        _outproj_kernel,
        grid=(nb, t // tm),
        in_specs=[_tok(tm, D_MODEL),
                  pl.BlockSpec((1, N_MOD, D_MODEL), lambda b, i: (b, 0, 0))]
                 + [_tok(tm, RW)] * 8
                 + [_full(bd.shape), _full(gn_w.shape), _full(gn_b.shape), _full(dsk.shape),
                    _full(snw.shape), _full(wout.shape), _full(gain.shape)],
        out_specs=_tok(tm, D_MODEL),
        out_shape=jax.ShapeDtypeStruct(x.shape, F32),
        compiler_params=_params("parallel", "parallel"),
        name="outproj",
    )(x, mod, yrf, yrb, bonus, gate_r, ysf, ysb, xbc_act, z, bd, gn_w, gn_b, dsk, snw, wout, gain)


def _ffn_kernel(x_ref, mod_ref, g_in, g_out, w1_ref, w2_ref, o_ref):
    x = x_ref[0]
    m = mod_ref[0]
    hb = (_rms(x, g_in[...]) * (1.0 + m[4:5, :]) + m[3:4, :]).astype(BF16)
    acc = jnp.zeros(x.shape, F32)
    for j in range(D_FF // D_MODEL):
        cols = slice(j * D_MODEL, (j + 1) * D_MODEL)
        a = jnp.maximum(jnp.dot(hb, w1_ref[:, cols], preferred_element_type=F32), 0.0)
        acc = acc + jnp.dot((a * a).astype(BF16), w2_ref[cols, :], preferred_element_type=F32)
    o_ref[0] = x + m[5:6, :] * _rms(acc, g_out[...])


def _ffn(x, mod, g_in, g_out, w1, w2, tm):
    nb, t, _ = x.shape
    return pl.pallas_call(
        _ffn_kernel,
        grid=(nb, t // tm),
        in_specs=[_tok(tm, D_MODEL),
                  pl.BlockSpec((1, N_MOD, D_MODEL), lambda b, i: (b, 0, 0)),
                  _full(g_in.shape), _full(g_out.shape), _full(w1.shape), _full(w2.shape)],
        out_specs=_tok(tm, D_MODEL),
        out_shape=jax.ShapeDtypeStruct(x.shape, F32),
        compiler_params=_params("parallel", "parallel"),
        name="ffn",
    )(x, mod, g_in, g_out, w1, w2)


def _block_diag2(w):
    z = jnp.zeros_like(w[0])
    return jnp.concatenate([jnp.concatenate([w[0], z], axis=1),
                            jnp.concatenate([z, w[1]], axis=1)], axis=0)


def _layer_weights(l, w):
    w_in = w["w_in"][l]
    ssm0 = RWKV_COLS
    p = {}
    p["wr"] = w_in[:, :RWKV_COLS].astype(BF16)
    p["wz"] = w_in[:, ssm0:ssm0 + SW].astype(BF16)
    p["wx"] = w_in[:, ssm0 + SW:ssm0 + SW + XBC_COLS].astype(BF16)
    wdt = w_in[:, ssm0 + SW + XBC_COLS:]
    p["wdt"] = wdt.astype(BF16)
    p["wdtt"] = wdt.T.astype(BF16)
    db = w["dt_bias"][l].reshape(1, -1)
    p["db"] = db
    p["dbt"] = db.reshape(-1, 1)
    p["mu"] = w["shift_mu"][l]
    p["w0"] = w["w0"][l].reshape(1, -1)
    p["wup"] = _block_diag2(w["w_up"][l]).astype(BF16)
    p["a0"] = w["a0"][l].reshape(1, -1)
    p["aup"] = _block_diag2(w["a_up"][l]).astype(BF16)
    p["gup"] = w["g_up"][l].astype(BF16)
    p["k_k"] = w["k_k"][l].reshape(1, -1)
    p["k_a"] = w["k_a"][l].reshape(1, -1)
    p["r_k"] = w["r_k"][l].reshape(1, -1)
    p["gn_w"] = w["gn_w"][l].reshape(1, -1)
    p["gn_b"] = w["gn_b"][l].reshape(1, -1)
    p["conv_w"] = w["conv_w"][l]
    p["conv_b"] = w["conv_b"][l].reshape(1, -1)
    p["a_log"] = w["A_log"][l]
    p["a_log_t"] = w["A_log"][l].reshape(-1, 1)
    p["dsk"] = jnp.repeat(w["d_skip"][l], HEAD_DIM).reshape(1, -1)
    p["snw"] = w["ssm_norm_w"][l].reshape(1, -1)
    p["wout"] = w["w_out"][l].astype(BF16)
    p["w1"] = w["w_ff1"][l].astype(BF16)
    p["w2"] = w["w_ff2"][l].astype(BF16)
    p["ng"] = [w["norm_g"][l, i].reshape(1, -1) for i in range(4)]
    return p


def _mixer_layer(x, mod, p, bd, tm):
    u_r, z, xbc, dt, dtt = _inproj(x, mod, p["ng"][0], p["wr"], p["wz"], p["wx"], p["wdt"],
                                   p["wdtt"], p["db"], p["dbt"], tm)
    r, k, v, kk, gate_r, bonus, lw, a = _rwkv_prep(u_r, p["mu"], p["w0"], p["wup"], p["a0"],
                                                   p["aup"], p["gup"], p["k_k"], p["r_k"], bd, tm)
    yrf, yrb = _rwkv_scan(r, k, v, kk, lw, a, p["k_a"])
    xbc_act = _ssd_prep(xbc, p["conv_w"], p["conv_b"], tm)
    ysf, ysb = _ssd_scan(xbc_act, dt, dtt, p["a_log"], p["a_log_t"])
    x = _outproj(x, mod, yrf, yrb, bonus, gate_r, ysf, ysb, xbc_act, z, bd, p["gn_w"], p["gn_b"],
                 p["dsk"], p["snw"], p["wout"], p["ng"][1], tm)
    return _ffn(x, mod, p["ng"][2], p["ng"][3], p["w1"], p["w2"], tm)


def _trunk(x, mods, layers, bd, tm):
    for l, p in enumerate(layers):
        x = _mixer_layer(x, mods[l], p, bd, tm)
    return x


def kernel(x_prompt, x_sample, c_prompt, c_sample, w_mod, b_mod, norm_g, w_in, shift_mu, w0, w_up, a0, a_up, g_up, k_k, k_a, r_k, gn_w, gn_b, conv_w, conv_b, dt_bias, A_log, d_skip, ssm_norm_w, w_out, w_ff1, w_ff2):
    w = dict(w_in=w_in, shift_mu=shift_mu, w0=w0, w_up=w_up, a0=a0, a_up=a_up, g_up=g_up, k_k=k_k,
             k_a=k_a, r_k=r_k, gn_w=gn_w, gn_b=gn_b, conv_w=conv_w, conv_b=conv_b, dt_bias=dt_bias,
             A_log=A_log, d_skip=d_skip, ssm_norm_w=ssm_norm_w, w_out=w_out, w_ff1=w_ff1,
             w_ff2=w_ff2, norm_g=norm_g)
    depth = w_mod.shape[0]
    n_p, n_s = c_prompt.shape[0], c_sample.shape[0]
    rows = -(-(n_p + n_s) // 8) * 8
    c_all = jnp.concatenate([c_prompt, c_sample, jnp.zeros((rows - n_p - n_s, D_MODEL), F32)], axis=0)
    mod = _modulation(c_all, w_mod, b_mod).reshape(depth, rows, N_MOD, D_MODEL)
    head = lax.broadcasted_iota(jnp.int32, (RW, RW), 0) // HEAD_DIM
    bd = (head == head.T).astype(F32)
    layers = [_layer_weights(l, w) for l in range(depth)]
    tm = 256
    y_prompt = _trunk(x_prompt, [mod[l, :n_p] for l in range(depth)], layers, bd, tm)
    y_sample = _trunk(x_sample, [mod[l, n_p:n_p + n_s] for l in range(depth)], layers, bd, tm)
    return (y_prompt, y_sample)
```

```python
import functools
import math

import jax
import jax.numpy as jnp
from jax import lax
from jax.experimental import pallas as pl
from jax.experimental.pallas import tpu as pltpu

F32 = jnp.float32
BF16 = jnp.bfloat16
HIGHEST = lax.Precision.HIGHEST

D_MODEL = 1024
HEAD_DIM = 64
RW = 512
N_HEADS = RW // HEAD_DIM
LORA = 64
GATE_LORA = 128
SW = 512
SSM_GROUPS = 2
D_STATE = 128
CONV_WIDTH = 5
SSD_CHUNK = 128
RWKV_CHUNK = 64
D_FF = 4 * D_MODEL
N_MOD = 6
NORM_EPS = 1e-6
GN_EPS = 64e-5
RWKV_COLS = 3 * RW + 2 * LORA + 2 * LORA + GATE_LORA
XBC_COLS = SW + 2 * SSM_GROUPS * D_STATE
HALO = 8
VMEM_LIMIT = 56 * 1024 * 1024
DECAY_SCALE = math.exp(-0.5)


def _bdot(a, b):
    return jnp.dot(a.astype(BF16), b.astype(BF16), preferred_element_type=F32)


def _bdot_nt(a, b):
    return lax.dot_general(a.astype(BF16), b.astype(BF16), (((1,), (1,)), ((), ())),
                           preferred_element_type=F32)


def _bdot_tn(a, b):
    return lax.dot_general(a.astype(BF16), b.astype(BF16), (((0,), (0,)), ((), ())),
                           preferred_element_type=F32)


def _fdot(a, b):
    return jnp.dot(a, b, precision=HIGHEST, preferred_element_type=F32)


def _split3(x):
    hi = x.astype(BF16)
    rest = x - hi.astype(F32)
    mid = rest.astype(BF16)
    return hi, mid, (rest - mid.astype(F32)).astype(BF16)


def _sel_dot(sel, x):
    sel = sel.astype(BF16)
    hi, mid, lo = _split3(x)
    return (jnp.dot(sel, hi, preferred_element_type=F32) + jnp.dot(sel, mid, preferred_element_type=F32)
            + jnp.dot(sel, lo, preferred_element_type=F32))


def _dot_sel_nt(x, sel):
    sel = sel.astype(BF16)
    dn = (((1,), (1,)), ((), ()))
    hi, mid, lo = _split3(x)
    return (lax.dot_general(hi, sel, dn, preferred_element_type=F32)
            + lax.dot_general(mid, sel, dn, preferred_element_type=F32)
            + lax.dot_general(lo, sel, dn, preferred_element_type=F32))


def _sigmoid(x):
    return 1.0 / (1.0 + jnp.exp(-x))


def _softplus(x):
    return jnp.maximum(x, 0.0) + jnp.log1p(jnp.exp(-jnp.abs(x)))


def _rms(x, gain):
    return x * lax.rsqrt(jnp.mean(x * x, axis=-1, keepdims=True) + NORM_EPS) * gain


def _params(*sem):
    return pltpu.CompilerParams(dimension_semantics=sem, vmem_limit_bytes=VMEM_LIMIT)


def _full(shape):
    return pl.BlockSpec(shape, lambda *_: (0,) * len(shape), pipeline_mode=pl.Buffered(1))


def _tok(tm, width, col=0):
    return pl.BlockSpec((1, tm, width), lambda b, i: (b, i, col))


def _halo_prev(tm, width):
    return pl.BlockSpec((1, HALO, width), lambda b, i: (b, jnp.maximum(i * (tm // HALO) - 1, 0), 0))


def _halo_next(tm, width, t):
    return pl.BlockSpec((1, HALO, width),
                        lambda b, i: (b, jnp.minimum((i + 1) * (tm // HALO), t // HALO - 1), 0))


def _mod_kernel(c_ref, w_ref, b_ref, o_ref):
    c = c_ref[...]
    o_ref[0] = _fdot(c * _sigmoid(c), w_ref[0]) + b_ref[0]


def _modulation(c_all, w_mod, b_mod):
    depth = w_mod.shape[0]
    rows = c_all.shape[0]
    nblk = N_MOD
    return pl.pallas_call(
        _mod_kernel,
        grid=(depth, nblk),
        in_specs=[pl.BlockSpec((rows, D_MODEL), lambda l, j: (0, 0)),
                  pl.BlockSpec((1, D_MODEL, D_MODEL), lambda l, j: (l, 0, j)),
                  pl.BlockSpec((1, 1, D_MODEL), lambda l, j: (l, 0, j))],
        out_specs=pl.BlockSpec((1, rows, D_MODEL), lambda l, j: (l, 0, j)),
        out_shape=jax.ShapeDtypeStruct((depth, rows, N_MOD * D_MODEL), F32),
        compiler_params=_params("arbitrary", "arbitrary"),
        name="modulation",
    )(c_all, w_mod, b_mod.reshape(depth, 1, N_MOD * D_MODEL))


def _inproj_kernel(x_ref, mod_ref, g_ref, wr_ref, wz_ref, wx_ref, wdt_ref, wdtt_ref, db_ref, dbt_ref,
                   ur_ref, z_ref, xbc_ref, dt_ref, dtt_ref):
    m = mod_ref[0]
    h = _rms(x_ref[0], g_ref[...]) * (1.0 + m[1:2, :]) + m[0:1, :]
    hb = h.astype(BF16)
    ur_ref[0] = jnp.dot(hb, wr_ref[...], preferred_element_type=F32)
    z_ref[0] = jnp.dot(hb, wz_ref[...], preferred_element_type=F32)
    xbc_ref[0] = jnp.dot(hb, wx_ref[...], preferred_element_type=F32)
    dt_ref[0] = _softplus(jnp.dot(hb, wdt_ref[...], preferred_element_type=F32) + db_ref[...])
    dtt_ref[0] = _softplus(_bdot_nt(wdtt_ref[...], hb) + dbt_ref[...])


def _inproj(x, mod, gain, wr, wz, wx, wdt, wdtt, db, dbt, tm):
    nb, t, _ = x.shape
    nd = wdt.shape[1]
    return pl.pallas_call(
        _inproj_kernel,
        grid=(nb, t // tm),
        in_specs=[_tok(tm, D_MODEL),
                  pl.BlockSpec((1, N_MOD, D_MODEL), lambda b, i: (b, 0, 0)),
                  _full((1, D_MODEL)), _full(wr.shape), _full(wz.shape), _full(wx.shape),
                  _full(wdt.shape), _full(wdtt.shape), _full(db.shape), _full(dbt.shape)],
        out_specs=[_tok(tm, RWKV_COLS), _tok(tm, SW), _tok(tm, XBC_COLS), _tok(tm, nd),
                   pl.BlockSpec((1, nd, tm), lambda b, i: (b, 0, i))],
        out_shape=[jax.ShapeDtypeStruct((nb, t, RWKV_COLS), F32),
                   jax.ShapeDtypeStruct((nb, t, SW), F32),
                   jax.ShapeDtypeStruct((nb, t, XBC_COLS), F32),
                   jax.ShapeDtypeStruct((nb, t, nd), F32),
                   jax.ShapeDtypeStruct((nb, nd, t), F32)],
        compiler_params=_params("parallel", "parallel"),
        name="inproj",
    )(x, mod, gain, wr, wz, wx, wdt, wdtt, db, dbt)


def _fill_halo(ext_ref, main_ref, prev_ref, next_ref, tm):
    i = pl.program_id(1)
    ext_ref[pl.ds(HALO, tm), :] = main_ref[0]
    ext_ref[pl.ds(0, HALO), :] = jnp.where(i > 0, prev_ref[0], 0.0)
    ext_ref[pl.ds(HALO + tm, HALO), :] = jnp.where(i < pl.num_programs(1) - 1, next_ref[0], 0.0)


def _rwkv_prep_kernel(u_ref, up_ref, un_ref, mu_ref, w0_ref, wup_ref, a0_ref, aup_ref, gup_ref,
                      kk_ref, rk_ref, bd_ref,
                      r_out, k_out, v_out, kkn_out, g_out, bonus_out, lw_out, a_out, ext_ref):
    tm = u_ref.shape[1]
    _fill_halo(ext_ref, u_ref, up_ref, un_ref, tm)
    u = ext_ref[pl.ds(HALO, tm), :]
    prev = ext_ref[pl.ds(HALO - 1, tm), :]
    nxt = ext_ref[pl.ds(HALO + 1, tm), :]
    mu = mu_ref[...]
    us = u + mu[0:1, :] * (prev - u) + mu[1:2, :] * (nxt - u)
    r = us[:, 0:RW]
    k = us[:, RW:2 * RW]
    v = us[:, 2 * RW:3 * RW]
    o = 3 * RW
    w_lo = us[:, o:o + 2 * LORA]
    a_lo = us[:, o + 2 * LORA:o + 4 * LORA]
    g_lo = us[:, o + 4 * LORA:o + 4 * LORA + GATE_LORA]
    w_raw = w0_ref[...] + _bdot(jnp.tanh(w_lo), wup_ref[...])
    lw_out[0] = -DECAY_SCALE * _sigmoid(w_raw)
    a_out[0] = _sigmoid(a0_ref[...] + _bdot(a_lo, aup_ref[...]))
    g_out[0] = _bdot(_sigmoid(g_lo), gup_ref[...])
    bd = bd_ref[...]
    kk = k * kk_ref[...]
    kk_norm = jnp.sqrt(_bdot(kk * kk, bd))
    kkn_out[0] = kk / jnp.maximum(kk_norm, 1e-12)
    bonus_out[0] = _bdot(r * k * rk_ref[...], bd) * v
    r_out[0] = r
    k_out[0] = k
    v_out[0] = v


def _rwkv_prep(u_r, mu, w0, wup, a0, aup, gup, k_k, r_k, bd, tm):
    nb, t, _ = u_r.shape
    wide = jax.ShapeDtypeStruct((nb, t, 2 * RW), F32)
    one = jax.ShapeDtypeStruct((nb, t, RW), F32)
    return pl.pallas_call(
        _rwkv_prep_kernel,
        grid=(nb, t // tm),
        in_specs=[_tok(tm, RWKV_COLS), _halo_prev(tm, RWKV_COLS), _halo_next(tm, RWKV_COLS, t),
                  _full(mu.shape), _full(w0.shape), _full(wup.shape), _full(a0.shape),
                  _full(aup.shape), _full(gup.shape), _full(k_k.shape), _full(r_k.shape),
                  _full(bd.shape)],
        out_specs=[_tok(tm, RW)] * 6 + [_tok(tm, 2 * RW)] * 2,
        out_shape=[one] * 6 + [wide] * 2,
        scratch_shapes=[pltpu.VMEM((tm + 2 * HALO, RWKV_COLS), F32)],
        compiler_params=_params("parallel", "parallel"),
        name="rwkv_prep",
    )(u_r, u_r, u_r, mu, w0, wup, a0, aup, gup, k_k, r_k, bd)


def _rwkv_scan_kernel(rf, kf, vf, kkf, lwf, af, rb, kb, vb, kkb, lwb, ab, ka_ref,
                      yf_ref, yb_ref, s_ref):
    @pl.when(pl.program_id(1) == 0)
    def _():
        s_ref[...] = jnp.zeros_like(s_ref)

    c = rf.shape[1]
    row = lax.broadcasted_iota(jnp.int32, (c, c), 0)
    col = lax.broadcasted_iota(jnp.int32, (c, c), 1)
    eye = (row == col).astype(F32)
    sibling = row ^ col
    ka = ka_ref[...]
    dirs = ((rf, kf, vf, kkf, lwf, af, yf_ref), (rb, kb, vb, kkb, lwb, ab, yb_ref))
    probs = []
    for d, (r_ref, k_ref, v_ref, kk_ref, lw_ref, a_ref, y_ref) in enumerate(dirs):
        incl = (col <= row) if d == 0 else (col >= row)
        strict = (col < row) if d == 0 else (col > row)
        last = c - 1 if d == 0 else 0
        r, k, v, kk, lw, a = r_ref[0], k_ref[0], v_ref[0], kk_ref[0], lw_ref[0], a_ref[0]
        cum = _sel_dot(incl, lw)
        cum_end = cum[last:last + 1, :]
        e_neg = jnp.exp(-cum)
        e_end = jnp.exp(cum_end - cum)
        kd = k * (1.0 + (a - 1.0) * ka)
        bb = kk * a
        at = -kk * jnp.exp(cum - lw)
        rt = r * jnp.exp(cum)
        bt = bb * e_neg
        kt = kd * e_neg
        bh = bb * e_end
        kh = kd * e_end
        p_end = jnp.exp(cum_end)
        for h in range(N_HEADS):
            sl = slice(h * HEAD_DIM, (h + 1) * HEAD_DIM)
            probs.append(dict(d=d, h=h, sl=sl, incl=incl, strict=strict, y_ref=y_ref,
                              lhs=jnp.concatenate([at[:, sl], rt[:, sl]], axis=0),
                              bt=bt[:, sl], kt=kt[:, sl], bh=bh[:, sl], kh=kh[:, sl],
                              v=v[:, sl], p_end=p_end[:, sl]))
    for q in probs:
        ab_ = _bdot_nt(q["lhs"], q["bt"])
        ak_ = _bdot_nt(q["lhs"], q["kt"])
        q["n_ab"] = jnp.where(q["strict"], ab_[:c], 0.0)
        q["a_ak"] = jnp.where(q["strict"], ak_[:c], 0.0)
        q["a_rb"] = jnp.where(q["incl"], ab_[c:], 0.0)
        q["a_rk"] = jnp.where(q["incl"], ak_[c:], 0.0)
        q["inv"] = eye + jnp.where(sibling == 1, q["n_ab"], 0.0)
    for lvl in range(1, int(math.log2(c))):
        for q in probs:
            q["md"] = _bdot(jnp.where((sibling >> lvl) == 1, q["n_ab"], 0.0), q["inv"])
        for q in probs:
            q["inv"] = q["inv"] + _bdot(q["inv"], q["md"])
    for q in probs:
        q["s0"] = s_ref[q["d"], q["h"]]
        q["sh"] = _bdot_nt(q["lhs"], q["s0"])
        q["rhs"] = q["sh"][:c] + _bdot(q["a_ak"], q["v"])
    for q in probs:
        q["u"] = _bdot(q["inv"], q["rhs"])
    for q in probs:
        q["y_ref"][0, :, q["sl"]] = q["sh"][c:] + _bdot(q["a_rb"], q["u"]) + _bdot(q["a_rk"], q["v"])
        s_ref[q["d"], q["h"]] = (q["s0"] * q["p_end"] + _bdot_tn(q["u"], q["bh"])
                                 + _bdot_tn(q["v"], q["kh"]))


def _rwkv_scan(r, k, v, kk, lw, a, k_a):
    nb, t, _ = r.shape
    c = RWKV_CHUNK
    nc = t // c
    fwd = pl.BlockSpec((1, c, RW), lambda b, i: (b, i, 0))
    bwd = pl.BlockSpec((1, c, RW), lambda b, i: (b, nc - 1 - i, 0))
    bwd1 = pl.BlockSpec((1, c, RW), lambda b, i: (b, nc - 1 - i, 1))
    out = jax.ShapeDtypeStruct((nb, t, RW), F32)
    return pl.pallas_call(
        _rwkv_scan_kernel,
        grid=(nb, nc),
        in_specs=[fwd] * 6 + [bwd] * 4 + [bwd1] * 2 + [_full(k_a.shape)],
        out_specs=[fwd, bwd],
        out_shape=[out, out],
        scratch_shapes=[pltpu.VMEM((2, N_HEADS, HEAD_DIM, HEAD_DIM), F32)],
        compiler_params=_params("parallel", "arbitrary"),
        name="rwkv_scan",
    )(r, k, v, kk, lw, a, r, k, v, kk, lw, a, k_a)


def _ssd_prep_kernel(u_ref, up_ref, un_ref, w_ref, b_ref, o_ref, ext_ref):
    tm = u_ref.shape[1]
    _fill_halo(ext_ref, u_ref, up_ref, un_ref, tm)
    w = w_ref[...]
    half = CONV_WIDTH // 2
    acc = b_ref[...] + w[0:1, :] * ext_ref[pl.ds(HALO - half, tm), :]
    for j in range(1, CONV_WIDTH):
        acc = acc + w[j:j + 1, :] * ext_ref[pl.ds(HALO - half + j, tm), :]
    o_ref[0] = acc * _sigmoid(acc)


def _ssd_prep(xbc, conv_w, conv_b, tm):
    nb, t, _ = xbc.shape
    return pl.pallas_call(
        _ssd_prep_kernel,
        grid=(nb, t // tm),
        in_specs=[_tok(tm, XBC_COLS), _halo_prev(tm, XBC_COLS), _halo_next(tm, XBC_COLS, t),
                  _full(conv_w.shape), _full(conv_b.shape)],
        out_specs=_tok(tm, XBC_COLS),
        out_shape=jax.ShapeDtypeStruct((nb, t, XBC_COLS), F32),
        scratch_shapes=[pltpu.VMEM((tm + 2 * HALO, XBC_COLS), F32)],
        compiler_params=_params("parallel", "parallel"),
        name="ssd_prep",
    )(xbc, xbc, xbc, conv_w, conv_b)


def _ssd_scan_kernel(xf, dtf, dttf, xb, dtb, dttb, alog_ref, alogt_ref, yf_ref, yb_ref, s_ref):
    @pl.when(pl.program_id(1) == 0)
    def _():
        s_ref[...] = jnp.zeros_like(s_ref)

    n = xf.shape[1]
    row = lax.broadcasted_iota(jnp.int32, (n, n), 0)
    col = lax.broadcasted_iota(jnp.int32, (n, n), 1)
    expand = (lax.broadcasted_iota(jnp.int32, (N_HEADS, SW), 1) // HEAD_DIM
              == lax.broadcasted_iota(jnp.int32, (N_HEADS, SW), 0)).astype(BF16)
    gw = SW // SSM_GROUPS
    hpg = N_HEADS // SSM_GROUPS
    dirs = ((xf, dtf, dttf, yf_ref), (xb, dtb, dttb, yb_ref))
    for d, (x_ref, dt_ref, dtt_ref, y_ref) in enumerate(dirs):
        incl = (col <= row) if d == 0 else (col >= row)
        last = n - 1 if d == 0 else 0
        xbc = x_ref[0]
        xs = xbc[:, 0:SW]
        dt = dt_ref[0][:, d * N_HEADS:(d + 1) * N_HEADS]
        dtt = dtt_ref[0][d * N_HEADS:(d + 1) * N_HEADS, :]
        a_row = -jnp.exp(alog_ref[d:d + 1, :])
        a_col = -jnp.exp(alogt_ref[d * N_HEADS:(d + 1) * N_HEADS, :])
        cum = _sel_dot(incl, dt * a_row)
        cum_t = _dot_sel_nt(dtt * a_col, incl)
        cum_end = cum[last:last + 1, :]
        dtx = _bdot(dt, expand)
        decx = _bdot(jnp.exp(cum_end - cum), expand)
        ecumx = _bdot(jnp.exp(cum), expand)
        xdt = xs * dtx
        xdec = xdt * decx
        for g in range(SSM_GROUPS):
            bg = xbc[:, SW + g * D_STATE:SW + (g + 1) * D_STATE]
            cg = xbc[:, SW + SSM_GROUPS * D_STATE + g * D_STATE:SW + SSM_GROUPS * D_STATE + (g + 1) * D_STATE]
            gl = slice(g * gw, (g + 1) * gw)
            cb = _bdot_nt(cg, bg)
            s0 = s_ref[d, g]
            y_off = _bdot(cg, s0) * ecumx[:, gl]
            for e in range(hpg):
                h = g * hpg + e
                sl = slice(h * HEAD_DIM, (h + 1) * HEAD_DIM)
                seg = cum[:, h:h + 1] - cum_t[h:h + 1, :]
                lmat = jnp.where(incl, jnp.exp(seg), 0.0)
                y_ref[0, :, sl] = _bdot(cb * lmat, xdt[:, sl]) + y_off[:, e * HEAD_DIM:(e + 1) * HEAD_DIM]
            s_ref[d, g] = s0 * ecumx[last:last + 1, gl] + _bdot_tn(bg, xdec[:, gl])


def _ssd_scan(xbc, dt, dtt, a_log, a_log_t):
    nb, t, _ = xbc.shape
    n = SSD_CHUNK
    nc = t // n
    nd = dt.shape[2]
    specs = []
    for idx in (lambda i: i, lambda i: nc - 1 - i):
        specs += [pl.BlockSpec((1, n, XBC_COLS), lambda b, i, f=idx: (b, f(i), 0)),
                  pl.BlockSpec((1, n, nd), lambda b, i, f=idx: (b, f(i), 0)),
                  pl.BlockSpec((1, nd, n), lambda b, i, f=idx: (b, 0, f(i)))]
    out = jax.ShapeDtypeStruct((nb, t, SW), F32)
    return pl.pallas_call(
        _ssd_scan_kernel,
        grid=(nb, nc),
        in_specs=specs + [_full(a_log.shape), _full(a_log_t.shape)],
        out_specs=[pl.BlockSpec((1, n, SW), lambda b, i: (b, i, 0)),
                   pl.BlockSpec((1, n, SW), lambda b, i: (b, nc - 1 - i, 0))],
        out_shape=[out, out],
        scratch_shapes=[pltpu.VMEM((2, SSM_GROUPS, D_STATE, SW // SSM_GROUPS), F32)],
        compiler_params=_params("parallel", "arbitrary"),
        name="ssd_scan",
    )(xbc, dt, dtt, xbc, dt, dtt, a_log, a_log_t)


def _outproj_kernel(x_ref, mod_ref, yrf, yrb, bonus, gate_r, ysf, ysb, xs, z, bd_ref, gnw, gnb,
                    dsk, snw, wout_ref, ng_ref, o_ref):
    bd = bd_ref[...]
    inv_n = 1.0 / HEAD_DIM
    y = yrf[0] + yrb[0]
    delta = y - _bdot(y, bd) * inv_n
    var = _bdot(delta * delta, bd) * inv_n
    y_r = (delta * lax.rsqrt(var + GN_EPS) * gnw[...] + gnb[...] + bonus[0]) * gate_r[0]
    zz = z[0]
    y_s = (ysf[0] + ysb[0] + dsk[...] * xs[0]) * (zz * _sigmoid(zz))
    y_s = _rms(y_s, snw[...])
    o = _bdot(y_r, wout_ref[0:RW, :]) + _bdot(y_s, wout_ref[RW:RW + SW, :])
    o_ref[0] = x_ref[0] + mod_ref[0][2:3, :] * _rms(o, ng_ref[...])


def _outproj(x, mod, yrf, yrb, bonus, gate_r, ysf, ysb, xbc_act, z, bd, gn_w, gn_b, dsk, snw,
             wout, gain, tm):
    nb, t, _ = x.shape
    return pl.pallas_call(
        _outproj_kernel,
        grid=(nb, t // tm),
        in_specs=[_tok(tm, D_MODEL),
                  pl.BlockSpec((1, N_MOD, D_MODEL), lambda b, i: (b, 0, 0))]
                 + [_tok(tm, RW)] * 8
                 + [_full(bd.shape), _full(gn_w.shape), _full(gn_b.shape), _full(dsk.shape),
                    _full(snw.shape), _full(wout.shape), _full(gain.shape)],
        out_specs=_tok(tm, D_MODEL),
        out_shape=jax.ShapeDtypeStruct(x.shape, F32),
        compiler_params=_params("parallel", "parallel"),
        name="outproj",
    )(x, mod, yrf, yrb, bonus, gate_r, ysf, ysb, xbc_act, z, bd, gn_w, gn_b, dsk, snw, wout, gain)


def _ffn_kernel(x_ref, mod_ref, g_in, g_out, w1_ref, w2_ref, o_ref):
    x = x_ref[0]
    m = mod_ref[0]
    hb = (_rms(x, g_in[...]) * (1.0 + m[4:5, :]) + m[3:4, :]).astype(BF16)
    acc = jnp.zeros(x.shape, F32)
    for j in range(D_FF // D_MODEL):
        cols = slice(j * D_MODEL, (j + 1) * D_MODEL)
        a = jnp.maximum(jnp.dot(hb, w1_ref[:, cols], preferred_element_type=F32), 0.0)
        acc = acc + jnp.dot((a * a).astype(BF16), w2_ref[cols, :], preferred_element_type=F32)
    o_ref[0] = x + m[5:6, :] * _rms(acc, g_out[...])


def _ffn(x, mod, g_in, g_out, w1, w2, tm):
    nb, t, _ = x.shape
    return pl.pallas_call(
        _ffn_kernel,
        grid=(nb, t // tm),
        in_specs=[_tok(tm, D_MODEL),
                  pl.BlockSpec((1, N_MOD, D_MODEL), lambda b, i: (b, 0, 0)),
                  _full(g_in.shape), _full(g_out.shape), _full(w1.shape), _full(w2.shape)],
        out_specs=_tok(tm, D_MODEL),
        out_shape=jax.ShapeDtypeStruct(x.shape, F32),
        compiler_params=_params("parallel", "parallel"),
        name="ffn",
    )(x, mod, g_in, g_out, w1, w2)


def _block_diag2(w):
    z = jnp.zeros_like(w[0])
    return jnp.concatenate([jnp.concatenate([w[0], z], axis=1),
                            jnp.concatenate([z, w[1]], axis=1)], axis=0)


def _layer_weights(l, w):
    w_in = w["w_in"][l]
    ssm0 = RWKV_COLS
    p = {}
    p["wr"] = w_in[:, :RWKV_COLS].astype(BF16)
    p["wz"] = w_in[:, ssm0:ssm0 + SW].astype(BF16)
    p["wx"] = w_in[:, ssm0 + SW:ssm0 + SW + XBC_COLS].astype(BF16)
    wdt = w_in[:, ssm0 + SW + XBC_COLS:]
    p["wdt"] = wdt.astype(BF16)
    p["wdtt"] = wdt.T.astype(BF16)
    db = w["dt_bias"][l].reshape(1, -1)
    p["db"] = db
    p["dbt"] = db.reshape(-1, 1)
    p["mu"] = w["shift_mu"][l]
    p["w0"] = w["w0"][l].reshape(1, -1)
    p["wup"] = _block_diag2(w["w_up"][l]).astype(BF16)
    p["a0"] = w["a0"][l].reshape(1, -1)
    p["aup"] = _block_diag2(w["a_up"][l]).astype(BF16)
    p["gup"] = w["g_up"][l].astype(BF16)
    p["k_k"] = w["k_k"][l].reshape(1, -1)
    p["k_a"] = w["k_a"][l].reshape(1, -1)
    p["r_k"] = w["r_k"][l].reshape(1, -1)
    p["gn_w"] = w["gn_w"][l].reshape(1, -1)
    p["gn_b"] = w["gn_b"][l].reshape(1, -1)
    p["conv_w"] = w["conv_w"][l]
    p["conv_b"] = w["conv_b"][l].reshape(1, -1)
    p["a_log"] = w["A_log"][l]
    p["a_log_t"] = w["A_log"][l].reshape(-1, 1)
    p["dsk"] = jnp.repeat(w["d_skip"][l], HEAD_DIM).reshape(1, -1)
    p["snw"] = w["ssm_norm_w"][l].reshape(1, -1)
    p["wout"] = w["w_out"][l].astype(BF16)
    p["w1"] = w["w_ff1"][l].astype(BF16)
    p["w2"] = w["w_ff2"][l].astype(BF16)
    p["ng"] = [w["norm_g"][l, i].reshape(1, -1) for i in range(4)]
    return p


def _mixer_layer(x, mod, p, bd, tm):
    u_r, z, xbc, dt, dtt = _inproj(x, mod, p["ng"][0], p["wr"], p["wz"], p["wx"], p["wdt"],
                                   p["wdtt"], p["db"], p["dbt"], tm)
    r, k, v, kk, gate_r, bonus, lw, a = _rwkv_prep(u_r, p["mu"], p["w0"], p["wup"], p["a0"],
                                                   p["aup"], p["gup"], p["k_k"], p["r_k"], bd, tm)
    yrf, yrb = _rwkv_scan(r, k, v, kk, lw, a, p["k_a"])
    xbc_act = _ssd_prep(xbc, p["conv_w"], p["conv_b"], tm)
    ysf, ysb = _ssd_scan(xbc_act, dt, dtt, p["a_log"], p["a_log_t"])
    x = _outproj(x, mod, yrf, yrb, bonus, gate_r, ysf, ysb, xbc_act, z, bd, p["gn_w"], p["gn_b"],
                 p["dsk"], p["snw"], p["wout"], p["ng"][1], tm)
    return _ffn(x, mod, p["ng"][2], p["ng"][3], p["w1"], p["w2"], tm)


def _trunk(x, mods, layers, bd, tm):
    for l, p in enumerate(layers):
        x = _mixer_layer(x, mods[l], p, bd, tm)
    return x


def kernel(x_prompt, x_sample, c_prompt, c_sample, w_mod, b_mod, norm_g, w_in, shift_mu, w0, w_up, a0, a_up, g_up, k_k, k_a, r_k, gn_w, gn_b, conv_w, conv_b, dt_bias, A_log, d_skip, ssm_norm_w, w_out, w_ff1, w_ff2):
    w = dict(w_in=w_in, shift_mu=shift_mu, w0=w0, w_up=w_up, a0=a0, a_up=a_up, g_up=g_up, k_k=k_k,
             k_a=k_a, r_k=r_k, gn_w=gn_w, gn_b=gn_b, conv_w=conv_w, conv_b=conv_b, dt_bias=dt_bias,
             A_log=A_log, d_skip=d_skip, ssm_norm_w=ssm_norm_w, w_out=w_out, w_ff1=w_ff1,
             w_ff2=w_ff2, norm_g=norm_g)
    depth = w_mod.shape[0]
    n_p, n_s = c_prompt.shape[0], c_sample.shape[0]
    rows = -(-(n_p + n_s) // 8) * 8
    c_all = jnp.concatenate([c_prompt, c_sample, jnp.zeros((rows - n_p - n_s, D_MODEL), F32)], axis=0)
    mod = _modulation(c_all, w_mod, b_mod).reshape(depth, rows, N_MOD, D_MODEL)
    head = lax.broadcasted_iota(jnp.int32, (RW, RW), 0) // HEAD_DIM
    bd = (head == head.T).astype(BF16)
    layers = [_layer_weights(l, w) for l in range(depth)]
    tm = 512
    y_prompt = _trunk(x_prompt, [mod[l, :n_p] for l in range(depth)], layers, bd, tm)
    y_sample = _trunk(x_sample, [mod[l, n_p:n_p + n_s] for l in range(depth)], layers, bd, tm)
    return (y_prompt, y_sample)
```

```python
import functools
import math

import jax
import jax.numpy as jnp
from jax import lax
from jax.experimental import pallas as pl
from jax.experimental.pallas import tpu as pltpu

F32 = jnp.float32
BF16 = jnp.bfloat16
HIGHEST = lax.Precision.HIGHEST

D_MODEL = 1024
HEAD_DIM = 64
RW = 512
N_HEADS = RW // HEAD_DIM
LORA = 64
GATE_LORA = 128
SW = 512
SSM_GROUPS = 2
D_STATE = 128
CONV_WIDTH = 5
SSD_CHUNK = 128
RWKV_CHUNK = 128
D_FF = 4 * D_MODEL
N_MOD = 6
NORM_EPS = 1e-6
GN_EPS = 64e-5
RWKV_COLS = 3 * RW + 2 * LORA + 2 * LORA + GATE_LORA
XBC_COLS = SW + 2 * SSM_GROUPS * D_STATE
HALO = 8
VMEM_LIMIT = 56 * 1024 * 1024
DECAY_SCALE = math.exp(-0.5)


def _bdot(a, b):
    return jnp.dot(a.astype(BF16), b.astype(BF16), preferred_element_type=F32)


def _bdot_nt(a, b):
    return lax.dot_general(a.astype(BF16), b.astype(BF16), (((1,), (1,)), ((), ())),
                           preferred_element_type=F32)


def _bdot_tn(a, b):
    return lax.dot_general(a.astype(BF16), b.astype(BF16), (((0,), (0,)), ((), ())),
                           preferred_element_type=F32)


def _fdot(a, b):
    return jnp.dot(a, b, precision=HIGHEST, preferred_element_type=F32)


def _split3(x):
    hi = x.astype(BF16)
    rest = x - hi.astype(F32)
    mid = rest.astype(BF16)
    return hi, mid, (rest - mid.astype(F32)).astype(BF16)


def _sel_dot(sel, x):
    sel = sel.astype(BF16)
    hi, mid, lo = _split3(x)
    return (jnp.dot(sel, hi, preferred_element_type=F32) + jnp.dot(sel, mid, preferred_element_type=F32)
            + jnp.dot(sel, lo, preferred_element_type=F32))


def _dot_sel_nt(x, sel):
    sel = sel.astype(BF16)
    dn = (((1,), (1,)), ((), ()))
    hi, mid, lo = _split3(x)
    return (lax.dot_general(hi, sel, dn, preferred_element_type=F32)
            + lax.dot_general(mid, sel, dn, preferred_element_type=F32)
            + lax.dot_general(lo, sel, dn, preferred_element_type=F32))


def _sigmoid(x):
    return 1.0 / (1.0 + jnp.exp(-x))


def _softplus(x):
    return jnp.maximum(x, 0.0) + jnp.log1p(jnp.exp(-jnp.abs(x)))


def _rms(x, gain):
    return x * lax.rsqrt(jnp.mean(x * x, axis=-1, keepdims=True) + NORM_EPS) * gain


def _params(*sem):
    return pltpu.CompilerParams(dimension_semantics=sem, vmem_limit_bytes=VMEM_LIMIT)


def _full(shape):
    return pl.BlockSpec(shape, lambda *_: (0,) * len(shape), pipeline_mode=pl.Buffered(1))


def _tok(tm, width, col=0):
    return pl.BlockSpec((1, tm, width), lambda b, i: (b, i, col))


def _halo_prev(tm, width):
    return pl.BlockSpec((1, HALO, width), lambda b, i: (b, jnp.maximum(i * (tm // HALO) - 1, 0), 0))


def _halo_next(tm, width, t):
    return pl.BlockSpec((1, HALO, width),
                        lambda b, i: (b, jnp.minimum((i + 1) * (tm // HALO), t // HALO - 1), 0))


def _mod_kernel(c_ref, w_ref, b_ref, o_ref):
    c = c_ref[...]
    o_ref[0] = _fdot(c * _sigmoid(c), w_ref[0]) + b_ref[0]


def _modulation(c_all, w_mod, b_mod):
    depth = w_mod.shape[0]
    rows = c_all.shape[0]
    nblk = N_MOD
    return pl.pallas_call(
        _mod_kernel,
        grid=(depth, nblk),
        in_specs=[pl.BlockSpec((rows, D_MODEL), lambda l, j: (0, 0)),
                  pl.BlockSpec((1, D_MODEL, D_MODEL), lambda l, j: (l, 0, j)),
                  pl.BlockSpec((1, 1, D_MODEL), lambda l, j: (l, 0, j))],
        out_specs=pl.BlockSpec((1, rows, D_MODEL), lambda l, j: (l, 0, j)),
        out_shape=jax.ShapeDtypeStruct((depth, rows, N_MOD * D_MODEL), F32),
        compiler_params=_params("arbitrary", "arbitrary"),
        name="modulation",
    )(c_all, w_mod, b_mod.reshape(depth, 1, N_MOD * D_MODEL))


def _inproj_kernel(x_ref, mod_ref, g_ref, wr_ref, wz_ref, wx_ref, wdt_ref, wdtt_ref, db_ref, dbt_ref,
                   ur_ref, z_ref, xbc_ref, dt_ref, dtt_ref):
    m = mod_ref[0]
    h = _rms(x_ref[0], g_ref[...]) * (1.0 + m[1:2, :]) + m[0:1, :]
    hb = h.astype(BF16)
    ur_ref[0] = jnp.dot(hb, wr_ref[...], preferred_element_type=F32)
    z_ref[0] = jnp.dot(hb, wz_ref[...], preferred_element_type=F32)
    xbc_ref[0] = jnp.dot(hb, wx_ref[...], preferred_element_type=F32)
    dt_ref[0] = _softplus(jnp.dot(hb, wdt_ref[...], preferred_element_type=F32) + db_ref[...])
    dtt_ref[0] = _softplus(_bdot_nt(wdtt_ref[...], hb) + dbt_ref[...])


def _inproj(x, mod, gain, wr, wz, wx, wdt, wdtt, db, dbt, tm):
    nb, t, _ = x.shape
    nd = wdt.shape[1]
    return pl.pallas_call(
        _inproj_kernel,
        grid=(nb, t // tm),
        in_specs=[_tok(tm, D_MODEL),
                  pl.BlockSpec((1, N_MOD, D_MODEL), lambda b, i: (b, 0, 0)),
                  _full((1, D_MODEL)), _full(wr.shape), _full(wz.shape), _full(wx.shape),
                  _full(wdt.shape), _full(wdtt.shape), _full(db.shape), _full(dbt.shape)],
        out_specs=[_tok(tm, RWKV_COLS), _tok(tm, SW), _tok(tm, XBC_COLS), _tok(tm, nd),
                   pl.BlockSpec((1, nd, tm), lambda b, i: (b, 0, i))],
        out_shape=[jax.ShapeDtypeStruct((nb, t, RWKV_COLS), F32),
                   jax.ShapeDtypeStruct((nb, t, SW), F32),
                   jax.ShapeDtypeStruct((nb, t, XBC_COLS), F32),
                   jax.ShapeDtypeStruct((nb, t, nd), F32),
                   jax.ShapeDtypeStruct((nb, nd, t), F32)],
        compiler_params=_params("parallel", "parallel"),
        name="inproj",
    )(x, mod, gain, wr, wz, wx, wdt, wdtt, db, dbt)


def _fill_halo(ext_ref, main_ref, prev_ref, next_ref, tm):
    i = pl.program_id(1)
    ext_ref[pl.ds(HALO, tm), :] = main_ref[0]
    ext_ref[pl.ds(0, HALO), :] = jnp.where(i > 0, prev_ref[0], 0.0)
    ext_ref[pl.ds(HALO + tm, HALO), :] = jnp.where(i < pl.num_programs(1) - 1, next_ref[0], 0.0)


def _rwkv_prep_kernel(u_ref, up_ref, un_ref, mu_ref, w0_ref, wup_ref, a0_ref, aup_ref, gup_ref,
                      kk_ref, rk_ref, bd_ref,
                      r_out, k_out, v_out, kkn_out, g_out, bonus_out, lw_out, a_out, ext_ref):
    tm = u_ref.shape[1]
    _fill_halo(ext_ref, u_ref, up_ref, un_ref, tm)
    u = ext_ref[pl.ds(HALO, tm), :]
    prev = ext_ref[pl.ds(HALO - 1, tm), :]
    nxt = ext_ref[pl.ds(HALO + 1, tm), :]
    mu = mu_ref[...]
    us = u + mu[0:1, :] * (prev - u) + mu[1:2, :] * (nxt - u)
    r = us[:, 0:RW]
    k = us[:, RW:2 * RW]
    v = us[:, 2 * RW:3 * RW]
    o = 3 * RW
    w_lo = us[:, o:o + 2 * LORA]
    a_lo = us[:, o + 2 * LORA:o + 4 * LORA]
    g_lo = us[:, o + 4 * LORA:o + 4 * LORA + GATE_LORA]
    w_raw = w0_ref[...] + _bdot(jnp.tanh(w_lo), wup_ref[...])
    lw_out[0] = -DECAY_SCALE * _sigmoid(w_raw)
    a_out[0] = _sigmoid(a0_ref[...] + _bdot(a_lo, aup_ref[...]))
    g_out[0] = _bdot(_sigmoid(g_lo), gup_ref[...])
    bd = bd_ref[...]
    kk = k * kk_ref[...]
    kk_norm = jnp.sqrt(_bdot(kk * kk, bd))
    kkn_out[0] = kk / jnp.maximum(kk_norm, 1e-12)
    bonus_out[0] = _bdot(r * k * rk_ref[...], bd) * v
    r_out[0] = r
    k_out[0] = k
    v_out[0] = v


def _rwkv_prep(u_r, mu, w0, wup, a0, aup, gup, k_k, r_k, bd, tm):
    nb, t, _ = u_r.shape
    wide = jax.ShapeDtypeStruct((nb, t, 2 * RW), F32)
    one = jax.ShapeDtypeStruct((nb, t, RW), F32)
    return pl.pallas_call(
        _rwkv_prep_kernel,
        grid=(nb, t // tm),
        in_specs=[_tok(tm, RWKV_COLS), _halo_prev(tm, RWKV_COLS), _halo_next(tm, RWKV_COLS, t),
                  _full(mu.shape), _full(w0.shape), _full(wup.shape), _full(a0.shape),
                  _full(aup.shape), _full(gup.shape), _full(k_k.shape), _full(r_k.shape),
                  _full(bd.shape)],
        out_specs=[_tok(tm, RW)] * 6 + [_tok(tm, 2 * RW)] * 2,
        out_shape=[one] * 6 + [wide] * 2,
        scratch_shapes=[pltpu.VMEM((tm + 2 * HALO, RWKV_COLS), F32)],
        compiler_params=_params("parallel", "parallel"),
        name="rwkv_prep",
    )(u_r, u_r, u_r, mu, w0, wup, a0, aup, gup, k_k, r_k, bd)


def _rwkv_scan_kernel(rf, kf, vf, kkf, lwf, af, rb, kb, vb, kkb, lwb, ab, ka_ref,
                      yf_ref, yb_ref, s_ref):
    @pl.when(pl.program_id(1) == 0)
    def _():
        s_ref[...] = jnp.zeros_like(s_ref)

    c = rf.shape[1]
    row = lax.broadcasted_iota(jnp.int32, (c, c), 0)
    col = lax.broadcasted_iota(jnp.int32, (c, c), 1)
    eye = (row == col).astype(F32)
    sibling = row ^ col
    ka = ka_ref[...]
    dirs = ((rf, kf, vf, kkf, lwf, af, yf_ref), (rb, kb, vb, kkb, lwb, ab, yb_ref))
    probs = []
    for d, (r_ref, k_ref, v_ref, kk_ref, lw_ref, a_ref, y_ref) in enumerate(dirs):
        incl = (col <= row) if d == 0 else (col >= row)
        strict = (col < row) if d == 0 else (col > row)
        last = c - 1 if d == 0 else 0
        r, k, v, kk, lw, a = r_ref[0], k_ref[0], v_ref[0], kk_ref[0], lw_ref[0], a_ref[0]
        cum = _sel_dot(incl, lw)
        cum_end = cum[last:last + 1, :]
        e_neg = jnp.exp(-cum)
        e_end = jnp.exp(cum_end - cum)
        kd = k * (1.0 + (a - 1.0) * ka)
        bb = kk * a
        at = -kk * jnp.exp(cum - lw)
        rt = r * jnp.exp(cum)
        bt = bb * e_neg
        kt = kd * e_neg
        bh = bb * e_end
        kh = kd * e_end
        p_end = jnp.exp(cum_end)
        for h in range(N_HEADS):
            sl = slice(h * HEAD_DIM, (h + 1) * HEAD_DIM)
            probs.append(dict(d=d, h=h, sl=sl, incl=incl, strict=strict, y_ref=y_ref,
                              lhs=jnp.concatenate([at[:, sl], rt[:, sl]], axis=0),
                              bt=bt[:, sl], kt=kt[:, sl], bh=bh[:, sl], kh=kh[:, sl],
                              v=v[:, sl], p_end=p_end[:, sl]))
    for q in probs:
        ab_ = _bdot_nt(q["lhs"], q["bt"])
        ak_ = _bdot_nt(q["lhs"], q["kt"])
        q["n_ab"] = jnp.where(q["strict"], ab_[:c], 0.0)
        q["a_ak"] = jnp.where(q["strict"], ak_[:c], 0.0)
        q["a_rb"] = jnp.where(q["incl"], ab_[c:], 0.0)
        q["a_rk"] = jnp.where(q["incl"], ak_[c:], 0.0)
        q["inv"] = eye + jnp.where(sibling == 1, q["n_ab"], 0.0)
    for lvl in range(1, int(math.log2(c))):
        for q in probs:
            q["md"] = _bdot(jnp.where((sibling >> lvl) == 1, q["n_ab"], 0.0), q["inv"])
        for q in probs:
            q["inv"] = q["inv"] + _bdot(q["inv"], q["md"])
    for q in probs:
        q["s0"] = s_ref[q["d"], q["h"]]
        q["sh"] = _bdot_nt(q["lhs"], q["s0"])
        q["rhs"] = q["sh"][:c] + _bdot(q["a_ak"], q["v"])
    for q in probs:
        q["u"] = _bdot(q["inv"], q["rhs"])
    for q in probs:
        q["y_ref"][0, :, q["sl"]] = q["sh"][c:] + _bdot(q["a_rb"], q["u"]) + _bdot(q["a_rk"], q["v"])
        s_ref[q["d"], q["h"]] = (q["s0"] * q["p_end"] + _bdot_tn(q["u"], q["bh"])
                                 + _bdot_tn(q["v"], q["kh"]))


def _rwkv_scan(r, k, v, kk, lw, a, k_a):
    nb, t, _ = r.shape
    c = RWKV_CHUNK
    nc = t // c
    fwd = pl.BlockSpec((1, c, RW), lambda b, i: (b, i, 0))
    bwd = pl.BlockSpec((1, c, RW), lambda b, i: (b, nc - 1 - i, 0))
    bwd1 = pl.BlockSpec((1, c, RW), lambda b, i: (b, nc - 1 - i, 1))
    out = jax.ShapeDtypeStruct((nb, t, RW), F32)
    return pl.pallas_call(
        _rwkv_scan_kernel,
        grid=(nb, nc),
        in_specs=[fwd] * 6 + [bwd] * 4 + [bwd1] * 2 + [_full(k_a.shape)],
        out_specs=[fwd, bwd],
        out_shape=[out, out],
        scratch_shapes=[pltpu.VMEM((2, N_HEADS, HEAD_DIM, HEAD_DIM), F32)],
        compiler_params=_params("parallel", "arbitrary"),
        name="rwkv_scan",
    )(r, k, v, kk, lw, a, r, k, v, kk, lw, a, k_a)


def _ssd_prep_kernel(u_ref, up_ref, un_ref, w_ref, b_ref, o_ref, ext_ref):
    tm = u_ref.shape[1]
    _fill_halo(ext_ref, u_ref, up_ref, un_ref, tm)
    w = w_ref[...]
    half = CONV_WIDTH // 2
    acc = b_ref[...] + w[0:1, :] * ext_ref[pl.ds(HALO - half, tm), :]
    for j in range(1, CONV_WIDTH):
        acc = acc + w[j:j + 1, :] * ext_ref[pl.ds(HALO - half + j, tm), :]
    o_ref[0] = acc * _sigmoid(acc)


def _ssd_prep(xbc, conv_w, conv_b, tm):
    nb, t, _ = xbc.shape
    return pl.pallas_call(
        _ssd_prep_kernel,
        grid=(nb, t // tm),
        in_specs=[_tok(tm, XBC_COLS), _halo_prev(tm, XBC_COLS), _halo_next(tm, XBC_COLS, t),
                  _full(conv_w.shape), _full(conv_b.shape)],
        out_specs=_tok(tm, XBC_COLS),
        out_shape=jax.ShapeDtypeStruct((nb, t, XBC_COLS), F32),
        scratch_shapes=[pltpu.VMEM((tm + 2 * HALO, XBC_COLS), F32)],
        compiler_params=_params("parallel", "parallel"),
        name="ssd_prep",
    )(xbc, xbc, xbc, conv_w, conv_b)


def _ssd_scan_kernel(xf, dtf, dttf, xb, dtb, dttb, alog_ref, alogt_ref, yf_ref, yb_ref, s_ref):
    @pl.when(pl.program_id(1) == 0)
    def _():
        s_ref[...] = jnp.zeros_like(s_ref)

    n = xf.shape[1]
    row = lax.broadcasted_iota(jnp.int32, (n, n), 0)
    col = lax.broadcasted_iota(jnp.int32, (n, n), 1)
    expand = (lax.broadcasted_iota(jnp.int32, (N_HEADS, SW), 1) // HEAD_DIM
              == lax.broadcasted_iota(jnp.int32, (N_HEADS, SW), 0)).astype(BF16)
    gw = SW // SSM_GROUPS
    hpg = N_HEADS // SSM_GROUPS
    dirs = ((xf, dtf, dttf, yf_ref), (xb, dtb, dttb, yb_ref))
    for d, (x_ref, dt_ref, dtt_ref, y_ref) in enumerate(dirs):
        incl = (col <= row) if d == 0 else (col >= row)
        last = n - 1 if d == 0 else 0
        xbc = x_ref[0]
        xs = xbc[:, 0:SW]
        dt = dt_ref[0][:, d * N_HEADS:(d + 1) * N_HEADS]
        dtt = dtt_ref[0][d * N_HEADS:(d + 1) * N_HEADS, :]
        a_row = -jnp.exp(alog_ref[d:d + 1, :])
        a_col = -jnp.exp(alogt_ref[d * N_HEADS:(d + 1) * N_HEADS, :])
        cum = _sel_dot(incl, dt * a_row)
        cum_t = _dot_sel_nt(dtt * a_col, incl)
        cum_end = cum[last:last + 1, :]
        dtx = _bdot(dt, expand)
        decx = _bdot(jnp.exp(cum_end - cum), expand)
        ecumx = _bdot(jnp.exp(cum), expand)
        xdt = xs * dtx
        xdec = xdt * decx
        for g in range(SSM_GROUPS):
            bg = xbc[:, SW + g * D_STATE:SW + (g + 1) * D_STATE]
            cg = xbc[:, SW + SSM_GROUPS * D_STATE + g * D_STATE:SW + SSM_GROUPS * D_STATE + (g + 1) * D_STATE]
            gl = slice(g * gw, (g + 1) * gw)
            cb = _bdot_nt(cg, bg)
            s0 = s_ref[d, g]
            y_off = _bdot(cg, s0) * ecumx[:, gl]
            for e in range(hpg):
                h = g * hpg + e
                sl = slice(h * HEAD_DIM, (h + 1) * HEAD_DIM)
                seg = cum[:, h:h + 1] - cum_t[h:h + 1, :]
                lmat = jnp.where(incl, jnp.exp(seg), 0.0)
                y_ref[0, :, sl] = _bdot(cb * lmat, xdt[:, sl]) + y_off[:, e * HEAD_DIM:(e + 1) * HEAD_DIM]
            s_ref[d, g] = s0 * ecumx[last:last + 1, gl] + _bdot_tn(bg, xdec[:, gl])


def _ssd_scan(xbc, dt, dtt, a_log, a_log_t):
    nb, t, _ = xbc.shape
    n = SSD_CHUNK
    nc = t // n
    nd = dt.shape[2]
    specs = []
    for idx in (lambda i: i, lambda i: nc - 1 - i):
        specs += [pl.BlockSpec((1, n, XBC_COLS), lambda b, i, f=idx: (b, f(i), 0)),
                  pl.BlockSpec((1, n, nd), lambda b, i, f=idx: (b, f(i), 0)),
                  pl.BlockSpec((1, nd, n), lambda b, i, f=idx: (b, 0, f(i)))]
    out = jax.ShapeDtypeStruct((nb, t, SW), F32)
    return pl.pallas_call(
        _ssd_scan_kernel,
        grid=(nb, nc),
        in_specs=specs + [_full(a_log.shape), _full(a_log_t.shape)],
        out_specs=[pl.BlockSpec((1, n, SW), lambda b, i: (b, i, 0)),
                   pl.BlockSpec((1, n, SW), lambda b, i: (b, nc - 1 - i, 0))],
        out_shape=[out, out],
        scratch_shapes=[pltpu.VMEM((2, SSM_GROUPS, D_STATE, SW // SSM_GROUPS), F32)],
        compiler_params=_params("parallel", "arbitrary"),
        name="ssd_scan",
    )(xbc, dt, dtt, xbc, dt, dtt, a_log, a_log_t)


def _outproj_kernel(x_ref, mod_ref, yrf, yrb, bonus, gate_r, ysf, ysb, xs, z, bd_ref, gnw, gnb,
                    dsk, snw, wout_ref, ng_ref, o_ref):
    bd = bd_ref[...]
    inv_n = 1.0 / HEAD_DIM
    y = yrf[0] + yrb[0]
    delta = y - _bdot(y, bd) * inv_n
    var = _bdot(delta * delta, bd) * inv_n
    y_r = (delta * lax.rsqrt(var + GN_EPS) * gnw[...] + gnb[...] + bonus[0]) * gate_r[0]
    zz = z[0]
    y_s = (ysf[0] + ysb[0] + dsk[...] * xs[0]) * (zz * _sigmoid(zz))
    y_s = _rms(y_s, snw[...])
    o = _bdot(y_r, wout_ref[0:RW, :]) + _bdot(y_s, wout_ref[RW:RW + SW, :])
    o_ref[0] = x_ref[0] + mod_ref[0][2:3, :] * _rms(o, ng_ref[...])


def _outproj(x, mod, yrf, yrb, bonus, gate_r, ysf, ysb, xbc_act, z, bd, gn_w, gn_b, dsk, snw,
             wout, gain, tm):
    nb, t, _ = x.shape
    return pl.pallas_call(
        _outproj_kernel,
        grid=(nb, t // tm),
        in_specs=[_tok(tm, D_MODEL),
                  pl.BlockSpec((1, N_MOD, D_MODEL), lambda b, i: (b, 0, 0))]
                 + [_tok(tm, RW)] * 8
                 + [_full(bd.shape), _full(gn_w.shape), _full(gn_b.shape), _full(dsk.shape),
                    _full(snw.shape), _full(wout.shape), _full(gain.shape)],
        out_specs=_tok(tm, D_MODEL),
        out_shape=jax.ShapeDtypeStruct(x.shape, F32),
        compiler_params=_params("parallel", "parallel"),
        name="outproj",
    )(x, mod, yrf, yrb, bonus, gate_r, ysf, ysb, xbc_act, z, bd, gn_w, gn_b, dsk, snw, wout, gain)


def _ffn_kernel(x_ref, mod_ref, g_in, g_out, w1_ref, w2_ref, o_ref):
    x = x_ref[0]
    m = mod_ref[0]
    hb = (_rms(x, g_in[...]) * (1.0 + m[4:5, :]) + m[3:4, :]).astype(BF16)
    acc = jnp.zeros(x.shape, F32)
    for j in range(D_FF // D_MODEL):
        cols = slice(j * D_MODEL, (j + 1) * D_MODEL)
        a = jnp.maximum(jnp.dot(hb, w1_ref[:, cols], preferred_element_type=F32), 0.0)
        acc = acc + jnp.dot((a * a).astype(BF16), w2_ref[cols, :], preferred_element_type=F32)
    o_ref[0] = x + m[5:6, :] * _rms(acc, g_out[...])


def _ffn(x, mod, g_in, g_out, w1, w2, tm):
    nb, t, _ = x.shape
    return pl.pallas_call(
        _ffn_kernel,
        grid=(nb, t // tm),
        in_specs=[_tok(tm, D_MODEL),
                  pl.BlockSpec((1, N_MOD, D_MODEL), lambda b, i: (b, 0, 0)),
                  _full(g_in.shape), _full(g_out.shape), _full(w1.shape), _full(w2.shape)],
        out_specs=_tok(tm, D_MODEL),
        out_shape=jax.ShapeDtypeStruct(x.shape, F32),
        compiler_params=_params("parallel", "parallel"),
        name="ffn",
    )(x, mod, g_in, g_out, w1, w2)


def _block_diag2(w):
    z = jnp.zeros_like(w[0])
    return jnp.concatenate([jnp.concatenate([w[0], z], axis=1),
                            jnp.concatenate([z, w[1]], axis=1)], axis=0)


def _layer_weights(l, w):
    w_in = w["w_in"][l]
    ssm0 = RWKV_COLS
    p = {}
    p["wr"] = w_in[:, :RWKV_COLS].astype(BF16)
    p["wz"] = w_in[:, ssm0:ssm0 + SW].astype(BF16)
    p["wx"] = w_in[:, ssm0 + SW:ssm0 + SW + XBC_COLS].astype(BF16)
    wdt = w_in[:, ssm0 + SW + XBC_COLS:]
    p["wdt"] = wdt.astype(BF16)
    p["wdtt"] = wdt.T.astype(BF16)
    db = w["dt_bias"][l].reshape(1, -1)
    p["db"] = db
    p["dbt"] = db.reshape(-1, 1)
    p["mu"] = w["shift_mu"][l]
    p["w0"] = w["w0"][l].reshape(1, -1)
    p["wup"] = _block_diag2(w["w_up"][l]).astype(BF16)
    p["a0"] = w["a0"][l].reshape(1, -1)
    p["aup"] = _block_diag2(w["a_up"][l]).astype(BF16)
    p["gup"] = w["g_up"][l].astype(BF16)
    p["k_k"] = w["k_k"][l].reshape(1, -1)
    p["k_a"] = w["k_a"][l].reshape(1, -1)
    p["r_k"] = w["r_k"][l].reshape(1, -1)
    p["gn_w"] = w["gn_w"][l].reshape(1, -1)
    p["gn_b"] = w["gn_b"][l].reshape(1, -1)
    p["conv_w"] = w["conv_w"][l]
    p["conv_b"] = w["conv_b"][l].reshape(1, -1)
    p["a_log"] = w["A_log"][l]
    p["a_log_t"] = w["A_log"][l].reshape(-1, 1)
    p["dsk"] = jnp.repeat(w["d_skip"][l], HEAD_DIM).reshape(1, -1)
    p["snw"] = w["ssm_norm_w"][l].reshape(1, -1)
    p["wout"] = w["w_out"][l].astype(BF16)
    p["w1"] = w["w_ff1"][l].astype(BF16)
    p["w2"] = w["w_ff2"][l].astype(BF16)
    p["ng"] = [w["norm_g"][l, i].reshape(1, -1) for i in range(4)]
    return p


def _mixer_layer(x, mod, p, bd, tm):
    u_r, z, xbc, dt, dtt = _inproj(x, mod, p["ng"][0], p["wr"], p["wz"], p["wx"], p["wdt"],
                                   p["wdtt"], p["db"], p["dbt"], tm)
    r, k, v, kk, gate_r, bonus, lw, a = _rwkv_prep(u_r, p["mu"], p["w0"], p["wup"], p["a0"],
                                                   p["aup"], p["gup"], p["k_k"], p["r_k"], bd, tm)
    yrf, yrb = _rwkv_scan(r, k, v, kk, lw, a, p["k_a"])
    xbc_act = _ssd_prep(xbc, p["conv_w"], p["conv_b"], tm)
    ysf, ysb = _ssd_scan(xbc_act, dt, dtt, p["a_log"], p["a_log_t"])
    x = _outproj(x, mod, yrf, yrb, bonus, gate_r, ysf, ysb, xbc_act, z, bd, p["gn_w"], p["gn_b"],
                 p["dsk"], p["snw"], p["wout"], p["ng"][1], tm)
    return _ffn(x, mod, p["ng"][2], p["ng"][3], p["w1"], p["w2"], tm)


def _trunk(x, mods, layers, bd, tm):
    for l, p in enumerate(layers):
        x = _mixer_layer(x, mods[l], p, bd, tm)
    return x


def kernel(x_prompt, x_sample, c_prompt, c_sample, w_mod, b_mod, norm_g, w_in, shift_mu, w0, w_up, a0, a_up, g_up, k_k, k_a, r_k, gn_w, gn_b, conv_w, conv_b, dt_bias, A_log, d_skip, ssm_norm_w, w_out, w_ff1, w_ff2):
    w = dict(w_in=w_in, shift_mu=shift_mu, w0=w0, w_up=w_up, a0=a0, a_up=a_up, g_up=g_up, k_k=k_k,
             k_a=k_a, r_k=r_k, gn_w=gn_w, gn_b=gn_b, conv_w=conv_w, conv_b=conv_b, dt_bias=dt_bias,
             A_log=A_log, d_skip=d_skip, ssm_norm_w=ssm_norm_w, w_out=w_out, w_ff1=w_ff1,
             w_ff2=w_ff2, norm_g=norm_g)
    depth = w_mod.shape[0]
    n_p, n_s = c_prompt.shape[0], c_sample.shape[0]
    rows = -(-(n_p + n_s) // 8) * 8
    c_all = jnp.concatenate([c_prompt, c_sample, jnp.zeros((rows - n_p - n_s, D_MODEL), F32)], axis=0)
    mod = _modulation(c_all, w_mod, b_mod).reshape(depth, rows, N_MOD, D_MODEL)
    head = lax.broadcasted_iota(jnp.int32, (RW, RW), 0) // HEAD_DIM
    bd = (head == head.T).astype(BF16)
    layers = [_layer_weights(l, w) for l in range(depth)]
    tm = 512
    y_prompt = _trunk(x_prompt, [mod[l, :n_p] for l in range(depth)], layers, bd, tm)
    y_sample = _trunk(x_sample, [mod[l, n_p:n_p + n_s] for l in range(depth)], layers, bd, tm)
    return (y_prompt, y_sample)
```
